```python
import jax, jax.numpy as jnp
from jax import lax
import numpy as np

D_MODEL = 2048
BATCH = 1
SEQ = 16384
DEPTH = 1
DEC_BATCH = 16
DEC_SEQ = 64
PAST_LEN = 2048

CHUNK = 64
MIX_WIDTH = D_MODEL
SB_HEAD_DIM = 128
SB_WIDTH = MIX_WIDTH // 2
SB_HEADS = SB_WIDTH // SB_HEAD_DIM
MLP_WIDTH = MIX_WIDTH - SB_WIDTH
MLP_GROUP_DIM = 128
MLP_GROUPS = MLP_WIDTH // MLP_GROUP_DIM
MLP_CHUNK = 128
IN_COLS = 3 * SB_WIDTH + 2 * MLP_WIDTH
D_FF = ((8 * D_MODEL // 3 + 255) // 256) * 256
PLE_DIM = 256
Q_BLOCK = 128
EPS = 1e-6

kernel_name = "hymba_stickbreak_gmlp_macaron_stream_step"


def _rmsnorm(x, g):
    xf = x.astype(jnp.float32)
    y = xf * lax.rsqrt(jnp.mean(xf * xf, axis=-1, keepdims=True) + EPS)
    return (y * g.astype(jnp.float32)).astype(x.dtype)


def _swiglu(x, wg, wu, wd):
    return (jax.nn.silu(x @ wg) * (x @ wu)) @ wd


def _sb_block(q, q_pos, k, v, k_pos):
    z = jnp.einsum("bqhd,bkhd->bhqk", q, k, preferred_element_type=jnp.float32)
    z = z.astype(jnp.float32) * (SB_HEAD_DIM ** -0.5)
    valid = k_pos[None, :] < q_pos[:, None]
    log_1mb = jnp.where(valid, jax.nn.log_sigmoid(-z), 0.0)
    tail = lax.cumsum(log_1mb, axis=3, reverse=True) - log_1mb
    a = jnp.where(valid, jnp.exp(jax.nn.log_sigmoid(z) + tail), 0.0)
    return jnp.einsum("bhqk,bkhd->bqhd", a.astype(v.dtype), v)


def _sb_prompt(q, k, v):
    B, S, H, Dh = q.shape
    nb = S // Q_BLOCK
    qb = q.reshape(B, nb, Q_BLOCK, H, Dh).transpose(1, 0, 2, 3, 4)
    pos = jnp.arange(S, dtype=jnp.int32)
    pb = pos.reshape(nb, Q_BLOCK)
    out = lax.map(lambda a: _sb_block(a[0], a[1], k, v, pos), (qb, pb))
    return out.transpose(1, 0, 2, 3, 4).reshape(B, S, H, Dh)


def _sb_sample(q, k, v, cache_k, cache_v):
    T = q.shape[1]
    past = cache_k.shape[1]
    kk = jnp.concatenate([cache_k.astype(k.dtype), k], axis=1)
    vv = jnp.concatenate([cache_v.astype(v.dtype), v], axis=1)
    q_pos = past + jnp.arange(T, dtype=jnp.int32)
    k_pos = jnp.arange(past + T, dtype=jnp.int32)
    return _sb_block(q, q_pos, kk, vv, k_pos)


def _chunk_mlp(u, zv, w_s, b_s):
    B, T, G, C = u.shape
    L = min(T, MLP_CHUNK)
    n = T // L
    mask = jnp.tril(jnp.ones((L, L), dtype=w_s.dtype))
    w = w_s[:, :L, :L] * mask
    zc = zv.reshape(B, n, L, G, C)
    mixed = jnp.einsum("gij,bnjgc->bnigc", w, zc) + jnp.transpose(b_s[:, :L])[None, None, :, :, None]
    return u * mixed.reshape(B, T, G, C)


def _layer(x, p, ffn1_norm, ffn1_wg, ffn1_wu, ffn1_wd, mix_norm, w_in, gmlp_v_norm, gmlp_ws, gmlp_bs,
           w_out, ffn2_norm, ffn2_wg, ffn2_wu, ffn2_wd, ple_norm, ple_w_gate, ple_w_proj,
           cache_k=None, cache_v=None):
    B, T, _ = x.shape
    h = x + 0.5 * _swiglu(_rmsnorm(x, ffn1_norm), ffn1_wg, ffn1_wu, ffn1_wd)
    n = _rmsnorm(h, mix_norm)
    q, k, v, u, zv = jnp.split(n @ w_in, [SB_WIDTH, 2 * SB_WIDTH, 3 * SB_WIDTH, 3 * SB_WIDTH + MLP_WIDTH], axis=-1)
    q = q.reshape(B, T, SB_HEADS, SB_HEAD_DIM)
    k = k.reshape(B, T, SB_HEADS, SB_HEAD_DIM)
    v = v.reshape(B, T, SB_HEADS, SB_HEAD_DIM)
    if cache_k is None:
        o_sb = _sb_prompt(q, k, v)
    else:
        o_sb = _sb_sample(q, k, v, cache_k, cache_v)
    u = jax.nn.gelu(u).reshape(B, T, MLP_GROUPS, MLP_GROUP_DIM)
    zn = _rmsnorm(jax.nn.gelu(zv), gmlp_v_norm).reshape(B, T, MLP_GROUPS, MLP_GROUP_DIM)
    o_mlp = _chunk_mlp(u, zn, gmlp_ws, gmlp_bs)
    o = jnp.concatenate([o_sb.reshape(B, T, SB_WIDTH), o_mlp.reshape(B, T, MLP_WIDTH)], axis=-1)
    h = h + o @ w_out
    h = h + 0.5 * _swiglu(_rmsnorm(h, ffn2_norm), ffn2_wg, ffn2_wu, ffn2_wd)
    gate = jax.nn.sigmoid(_rmsnorm(h, ple_norm) @ ple_w_gate)
    h = h + gate * (p @ ple_w_proj)
    return h, k, v, zn


def setup_inputs(seed: int = 0) -> dict:
    key = jax.random.key(seed)
    ks = jax.random.split(key, 32)
    f32 = jnp.float32

    def nrm(k, shape, scale):
        return jax.random.normal(k, shape, f32) * scale

    def gain(k, shape):
        return 1.0 + 0.02 * jax.random.normal(k, shape, f32)

    return {
        "x_prompt": nrm(ks[0], (BATCH, SEQ, D_MODEL), 1.0),
        "x_sample": nrm(ks[1], (DEC_BATCH, DEC_SEQ, D_MODEL), 1.0),
        "cache_k": nrm(ks[2], (DEPTH, DEC_BATCH, PAST_LEN, SB_HEADS, SB_HEAD_DIM), 1.0),
        "cache_v": nrm(ks[3], (DEPTH, DEC_BATCH, PAST_LEN, SB_HEADS, SB_HEAD_DIM), 1.0),
        "p_prompt": nrm(ks[4], (DEPTH, BATCH, SEQ, PLE_DIM), 1.0),
        "p_sample": nrm(ks[5], (DEPTH, DEC_BATCH, DEC_SEQ, PLE_DIM), 1.0),
        "ffn1_norm": gain(ks[6], (DEPTH, D_MODEL)),
        "ffn1_wg": nrm(ks[7], (DEPTH, D_MODEL, D_FF), D_MODEL ** -0.5),
        "ffn1_wu": nrm(ks[8], (DEPTH, D_MODEL, D_FF), D_MODEL ** -0.5),
        "ffn1_wd": nrm(ks[9], (DEPTH, D_FF, D_MODEL), D_FF ** -0.5),
        "mix_norm": gain(ks[10], (DEPTH, D_MODEL)),
        "w_in": nrm(ks[11], (DEPTH, D_MODEL, IN_COLS), D_MODEL ** -0.5),
        "gmlp_v_norm": gain(ks[12], (DEPTH, MLP_WIDTH)),
        "gmlp_ws": nrm(ks[13], (DEPTH, MLP_GROUPS, MLP_CHUNK, MLP_CHUNK), MLP_CHUNK ** -0.5),
        "gmlp_bs": gain(ks[14], (DEPTH, MLP_GROUPS, MLP_CHUNK)),
        "w_out": nrm(ks[15], (DEPTH, MIX_WIDTH, D_MODEL), MIX_WIDTH ** -0.5),
        "ffn2_norm": gain(ks[16], (DEPTH, D_MODEL)),
        "ffn2_wg": nrm(ks[17], (DEPTH, D_MODEL, D_FF), D_MODEL ** -0.5),
        "ffn2_wu": nrm(ks[18], (DEPTH, D_MODEL, D_FF), D_MODEL ** -0.5),
        "ffn2_wd": nrm(ks[19], (DEPTH, D_FF, D_MODEL), D_FF ** -0.5),
        "ple_norm": gain(ks[20], (DEPTH, D_MODEL)),
        "ple_w_gate": nrm(ks[21], (DEPTH, D_MODEL, D_MODEL), D_MODEL ** -0.5),
        "ple_w_proj": nrm(ks[22], (DEPTH, PLE_DIM, D_MODEL), PLE_DIM ** -0.5),
        "final_norm": gain(ks[23], (D_MODEL,)),
    }


def reference(x_prompt, x_sample, cache_k, cache_v, p_prompt, p_sample,
              ffn1_norm, ffn1_wg, ffn1_wu, ffn1_wd, mix_norm, w_in, gmlp_v_norm, gmlp_ws, gmlp_bs,
              w_out, ffn2_norm, ffn2_wg, ffn2_wu, ffn2_wd, ple_norm, ple_w_gate, ple_w_proj, final_norm):
    hp, hs = x_prompt, x_sample
    kp_l, vp_l, ks_l, vs_l, zs_l = [], [], [], [], []
    for i in range(DEPTH):
        w = (ffn1_norm[i], ffn1_wg[i], ffn1_wu[i], ffn1_wd[i], mix_norm[i], w_in[i], gmlp_v_norm[i],
             gmlp_ws[i], gmlp_bs[i], w_out[i], ffn2_norm[i], ffn2_wg[i], ffn2_wu[i], ffn2_wd[i],
             ple_norm[i], ple_w_gate[i], ple_w_proj[i])
        hp, kp, vp, _ = _layer(hp, p_prompt[i], *w)
        hs, k_s, v_s, z_s = _layer(hs, p_sample[i], *w, cache_k=cache_k[i], cache_v=cache_v[i])
        kp_l.append(kp)
        vp_l.append(vp)
        ks_l.append(k_s)
        vs_l.append(v_s)
        zs_l.append(z_s)
    y_prompt = _rmsnorm(hp, final_norm)
    y_sample = _rmsnorm(hs, final_norm)
    state_k_prompt = jnp.stack(kp_l, axis=0)
    state_v_prompt = jnp.stack(vp_l, axis=0)
    state_k_sample = jnp.stack(ks_l, axis=0)
    state_v_sample = jnp.stack(vs_l, axis=0)
    state_mlpv_sample = jnp.stack(zs_l, axis=0)
    return (y_prompt, y_sample, state_k_prompt, state_v_prompt, state_k_sample, state_v_sample, state_mlpv_sample)
```

```python
import functools

import jax
import jax.numpy as jnp
from jax import lax
from jax.experimental import pallas as pl
from jax.experimental.pallas import tpu as pltpu

F32 = jnp.float32
BF16 = jnp.bfloat16

EPS = 1e-6
SB_HEAD_DIM = 128
MLP_GROUP_DIM = 128
MLP_CHUNK = 128

V7X_VMEM_BYTES = 64 * 1024 * 1024
VMEM_REQUEST_CAP = V7X_VMEM_BYTES - 6 * 1024 * 1024
LANES = 128

LOG_F32_UNDERFLOW = -105.0

ROW_TILE = 512
FFN_ROW_TILE = 512
FFN_COL_TILE = 512
ATT_BLOCK = 256


def _compiler_params(semantics, vmem_bytes):
    limit = min(int(vmem_bytes * 1.2) + (4 << 20), VMEM_REQUEST_CAP)
    return pltpu.CompilerParams(dimension_semantics=semantics, vmem_limit_bytes=limit)


def _rms_scale(x):
    return lax.rsqrt(jnp.mean(x * x, axis=-1, keepdims=True) + EPS)


def _ffn_kernel(x_ref, gain_ref, wg_ref, wu_ref, wd_ref, o_ref, n_ref):
    j = pl.program_id(1)

    @pl.when(j == 0)
    def _():
        x = x_ref[...]
        n_ref[...] = (x * _rms_scale(x) * gain_ref[...]).astype(BF16)
        o_ref[...] = x

    n = n_ref[...]
    g = jnp.dot(n, wg_ref[...], preferred_element_type=F32)
    u = jnp.dot(n, wu_ref[...], preferred_element_type=F32)
    a = (0.5 * g * jax.nn.sigmoid(g) * u).astype(BF16)
    o_ref[...] += jnp.dot(a, wd_ref[...], preferred_element_type=F32)


def _ffn(x, gain, wg, wu, wd):
    t, d = x.shape
    f = wg.shape[1]
    tm = min(FFN_ROW_TILE, t)
    tf = FFN_COL_TILE
    assert t % tm == 0 and f % tf == 0
    vmem = 2 * 2 * tm * d * 4 + tm * d * 2 + 2 * 3 * d * tf * 2 + 3 * tm * tf * 4
    return pl.pallas_call(
        _ffn_kernel,
        grid=(t // tm, f // tf),
        in_specs=[
            pl.BlockSpec((tm, d), lambda i, j: (i, 0)),
            pl.BlockSpec((1, d), lambda i, j: (0, 0)),
            pl.BlockSpec((d, tf), lambda i, j: (0, j)),
            pl.BlockSpec((d, tf), lambda i, j: (0, j)),
            pl.BlockSpec((tf, d), lambda i, j: (j, 0)),
        ],
        out_specs=pl.BlockSpec((tm, d), lambda i, j: (i, 0)),
        out_shape=jax.ShapeDtypeStruct((t, d), F32),
        scratch_shapes=[pltpu.VMEM((tm, d), BF16)],
        compiler_params=_compiler_params(("parallel", "arbitrary"), vmem),
        name="ffn",
    )(x, gain.reshape(1, d), wg, wu, wd)


def _inproj_kernel(h_ref, gain_ref, w_ref, vgain_ref, ws_ref, bst_ref,
                   q_ref, k_ref, v_ref, kb_ref, vb_ref, om_ref, *rest, emit_zn):
    if emit_zn:
        zn_out_ref, n_ref, u_ref = rest
    else:
        n_ref, u_ref = rest
    j = pl.program_id(1)

    @pl.when(j == 0)
    def _():
        h = h_ref[...]
        n_ref[...] = (h * _rms_scale(h) * gain_ref[...]).astype(BF16)

    y = jnp.dot(n_ref[...], w_ref[...], preferred_element_type=F32)

    @pl.when(j == 0)
    def _():
        q_ref[...] = (y * (SB_HEAD_DIM ** -0.5)).astype(BF16)

    @pl.when(j == 1)
    def _():
        k_ref[...] = y
        kb_ref[...] = y.astype(BF16)

    @pl.when(j == 2)
    def _():
        v_ref[...] = y
        vb_ref[...] = y.astype(BF16)

    @pl.when(j == 3)
    def _():
        u_ref[...] = jax.nn.gelu(y, approximate=True)

    @pl.when(j == 4)
    def _():
        gz = jax.nn.gelu(y, approximate=True)
        zn = gz * _rms_scale(gz) * vgain_ref[...]
        if emit_zn:
            zn_out_ref[...] = zn
        znb = zn.astype(BF16)
        tm = znb.shape[0]
        groups = ws_ref.shape[0]
        row = lax.broadcasted_iota(jnp.int32, (MLP_CHUNK, MLP_CHUNK), 0)
        col = lax.broadcasted_iota(jnp.int32, (MLP_CHUNK, MLP_CHUNK), 1)
        lower = col <= row
        for g in range(groups):
            w = jnp.where(lower, ws_ref[g], 0.0).astype(BF16)
            bias = bst_ref[:, g:g + 1]
            cs = slice(g * MLP_GROUP_DIM, (g + 1) * MLP_GROUP_DIM)
            for c in range(tm // MLP_CHUNK):
                rs = slice(c * MLP_CHUNK, (c + 1) * MLP_CHUNK)
                mixed = jnp.dot(w, znb[rs, cs], preferred_element_type=F32) + bias
                om_ref[rs, cs] = (u_ref[rs, cs] * mixed).astype(BF16)


def _inproj(h, gain, w_in, vgain, ws, bst, *, emit_zn):
    t, d = h.shape
    width = w_in.shape[1] // 5
    tm = min(ROW_TILE, t)
    assert t % tm == 0 and tm % MLP_CHUNK == 0 and w_in.shape[1] == 5 * width
    groups = ws.shape[0]
    row_blk = lambda i, j: (i, 0)
    const2 = lambda i, j: (0, 0)
    out_shape = [
        jax.ShapeDtypeStruct((t, width), BF16),
        jax.ShapeDtypeStruct((t, width), F32),
        jax.ShapeDtypeStruct((t, width), F32),
        jax.ShapeDtypeStruct((t, width), BF16),
        jax.ShapeDtypeStruct((t, width), BF16),
        jax.ShapeDtypeStruct((t, width), BF16),
    ]
    if emit_zn:
        out_shape.append(jax.ShapeDtypeStruct((t, width), F32))
    out_specs = [pl.BlockSpec((tm, width), row_blk) for _ in out_shape]
    out_bytes = sum(tm * width * jnp.dtype(s.dtype).itemsize for s in out_shape)
    vmem = (2 * tm * d * 4 + 2 * d * width * 2 + 2 * out_bytes + tm * d * 2
            + tm * width * 4 + 4 * tm * width * 4)
    return pl.pallas_call(
        functools.partial(_inproj_kernel, emit_zn=emit_zn),
        grid=(t // tm, 5),
        in_specs=[
            pl.BlockSpec((tm, d), row_blk),
            pl.BlockSpec((1, d), const2),
            pl.BlockSpec((d, width), lambda i, j: (0, j)),
            pl.BlockSpec((1, width), const2),
            pl.BlockSpec((groups, MLP_CHUNK, MLP_CHUNK), lambda i, j: (0, 0, 0)),
            pl.BlockSpec((MLP_CHUNK, groups), const2),
        ],
        out_specs=out_specs,
        out_shape=out_shape,
        scratch_shapes=[pltpu.VMEM((tm, d), BF16), pltpu.VMEM((tm, width), F32)],
        compiler_params=_compiler_params(("parallel", "arbitrary"), vmem),
        name="inproj",
    )(h, gain.reshape(1, d), w_in, vgain.reshape(1, width), ws, bst)


def _init_tri(tri_ref):
    tk = tri_ref.shape[1]
    j = lax.broadcasted_iota(jnp.int32, (2 * tk, tk), 0)
    s = lax.broadcasted_iota(jnp.int32, (2 * tk, tk), 1)
    j = jnp.where(j >= tk, j - tk, j)
    tri_ref[...] = jnp.where(j > s, 1.0, 0.0).astype(BF16)


def _sb_block(q, k, v, valid, tri_ref, carry_ref, acc_ref):
    z = lax.dot_general(q, k, (((1,), (1,)), ((), ())), preferred_element_type=F32)
    lp = jnp.log(1.0 + jnp.exp(-jnp.abs(z)))
    l = -(jnp.maximum(z, 0.0) + lp)
    if valid is not None:
        l = jnp.where(valid, l, 0.0)
    l_hi = l.astype(BF16)
    l_lo = (l - l_hi.astype(F32)).astype(BF16)
    inner = jnp.dot(jnp.concatenate([l_hi, l_lo], axis=1), tri_ref[...],
                    preferred_element_type=F32)
    a = jnp.exp(z + l + inner + carry_ref[...])
    if valid is not None:
        a = jnp.where(valid, a, 0.0)
    acc_ref[...] += jnp.dot(a.astype(BF16), v, preferred_element_type=F32)
    carry_ref[...] += jnp.sum(l, axis=1, keepdims=True)


def _attn_prompt_kernel(q_ref, k_ref, v_ref, o_ref, tri_ref, carry_ref, acc_ref):
    i = pl.program_id(1)
    tb = q_ref.shape[0]

    @pl.when((pl.program_id(0) == 0) & (i == 0))
    def _():
        _init_tri(tri_ref)

    carry_ref[...] = jnp.zeros_like(carry_ref)
    acc_ref[...] = jnp.zeros_like(acc_ref)
    q = q_ref[...]

    start = pl.multiple_of(i * tb, tb)
    row = lax.broadcasted_iota(jnp.int32, (tb, tb), 0)
    col = lax.broadcasted_iota(jnp.int32, (tb, tb), 1)
    _sb_block(q, k_ref[pl.ds(start, tb), :], v_ref[pl.ds(start, tb), :], col < row,
              tri_ref, carry_ref, acc_ref)

    def cond(state):
        kb, top = state
        return (kb >= 0) & (top > LOG_F32_UNDERFLOW)

    def body(state):
        kb, _ = state
        s = pl.multiple_of(kb * tb, tb)
        _sb_block(q, k_ref[pl.ds(s, tb), :], v_ref[pl.ds(s, tb), :], None,
                  tri_ref, carry_ref, acc_ref)
        return kb - 1, jnp.max(carry_ref[...])

    lax.while_loop(cond, body, (i - 1, jnp.max(carry_ref[...])))
    o_ref[...] = acc_ref[...].astype(o_ref.dtype)


def _attn_prompt(q, k, v):
    s, width = q.shape
    heads = width // SB_HEAD_DIM
    tb = ATT_BLOCK
    assert s % tb == 0
    vmem = 2 * 2 * s * SB_HEAD_DIM * 2 + 2 * tb * tb * 2 + 12 * tb * tb * 4
    return pl.pallas_call(
        _attn_prompt_kernel,
        grid=(heads, s // tb),
        in_specs=[
            pl.BlockSpec((tb, SB_HEAD_DIM), lambda h, i: (i, h)),
            pl.BlockSpec((s, SB_HEAD_DIM), lambda h, i: (0, h)),
            pl.BlockSpec((s, SB_HEAD_DIM), lambda h, i: (0, h)),
        ],
        out_specs=pl.BlockSpec((tb, SB_HEAD_DIM), lambda h, i: (i, h)),
        out_shape=jax.ShapeDtypeStruct((s, width), BF16),
        scratch_shapes=[
            pltpu.VMEM((2 * tb, tb), BF16),
            pltpu.VMEM((tb, 1), F32),
            pltpu.VMEM((tb, SB_HEAD_DIM), F32),
        ],
        compiler_params=_compiler_params(("arbitrary", "arbitrary"), vmem),
        name="attn_prompt",
    )(q, k, v)


def _attn_sample_kernel(q_ref, kn_ref, vn_ref, ck_ref, cv_ref, o_ref,
                        tri_ref, carry_ref, acc_ref):
    tq = q_ref.shape[0]
    past = ck_ref.shape[1]
    tb = tri_ref.shape[1]
    head = tb - tq
    nfull = (past - head) // tb
    rem = past - head - nfull * tb

    @pl.when((pl.program_id(0) == 0) & (pl.program_id(1) == 0))
    def _():
        _init_tri(tri_ref)

    carry_ref[...] = jnp.zeros_like(carry_ref)
    acc_ref[...] = jnp.zeros_like(acc_ref)
    q = q_ref[...]

    row = lax.broadcasted_iota(jnp.int32, (tq, tb), 0)
    col = lax.broadcasted_iota(jnp.int32, (tq, tb), 1)
    k0 = jnp.concatenate([ck_ref[0, past - head:past, :].astype(BF16), kn_ref[...]], axis=0)
    v0 = jnp.concatenate([cv_ref[0, past - head:past, :].astype(BF16), vn_ref[...]], axis=0)
    _sb_block(q, k0, v0, col < row + head, tri_ref, carry_ref, acc_ref)

    def cond(state):
        kb, top = state
        return (kb >= 0) & (top > LOG_F32_UNDERFLOW)

    def body(state):
        kb, _ = state
        s = pl.multiple_of(rem + kb * tb, 8)
        _sb_block(q, ck_ref[0, pl.ds(s, tb), :].astype(BF16),
                  cv_ref[0, pl.ds(s, tb), :].astype(BF16), None,
                  tri_ref, carry_ref, acc_ref)
        return kb - 1, jnp.max(carry_ref[...])

    _, top = lax.while_loop(cond, body, (nfull - 1, jnp.max(carry_ref[...])))

    if rem:
        @pl.when(top > LOG_F32_UNDERFLOW)
        def _():
            _sb_block(q, ck_ref[0, 0:tb, :].astype(BF16), cv_ref[0, 0:tb, :].astype(BF16),
                      col < rem, tri_ref, carry_ref, acc_ref)

    o_ref[...] = acc_ref[...].astype(o_ref.dtype)


def _attn_sample(q, kn, vn, cache_k, cache_v, batch):
    rows, width = q.shape
    heads = width // SB_HEAD_DIM
    tq = rows // batch
    past = cache_k.shape[1]
    tb = ATT_BLOCK
    assert tq < tb and past >= 2 * tb and tq % 16 == 0 and (past - (tb - tq)) % 8 == 0
    new_blk = lambda b, h: (b, h)
    cache_blk = lambda b, h: (b, 0, h)
    vmem = 2 * 2 * past * SB_HEAD_DIM * 4 + 2 * tb * tb * 2 + 12 * tq * tb * 4
    return pl.pallas_call(
        _attn_sample_kernel,
        grid=(batch, heads),
        in_specs=[
            pl.BlockSpec((tq, SB_HEAD_DIM), new_blk),
            pl.BlockSpec((tq, SB_HEAD_DIM), new_blk),
            pl.BlockSpec((tq, SB_HEAD_DIM), new_blk),
            pl.BlockSpec((1, past, SB_HEAD_DIM), cache_blk),
            pl.BlockSpec((1, past, SB_HEAD_DIM), cache_blk),
        ],
        out_specs=pl.BlockSpec((tq, SB_HEAD_DIM), new_blk),
        out_shape=jax.ShapeDtypeStruct((rows, width), BF16),
        scratch_shapes=[
            pltpu.VMEM((2 * tb, tb), BF16),
            pltpu.VMEM((tq, 1), F32),
            pltpu.VMEM((tq, SB_HEAD_DIM), F32),
        ],
        compiler_params=_compiler_params(("arbitrary", "arbitrary"), vmem),
        name="attn_sample",
    )(q, kn, vn, cache_k, cache_v)


def _outproj_kernel(h_ref, osb_ref, om_ref, wa_ref, wb_ref, o_ref):
    o_ref[...] = (h_ref[...]
                  + jnp.dot(osb_ref[...], wa_ref[...], preferred_element_type=F32)
                  + jnp.dot(om_ref[...], wb_ref[...], preferred_element_type=F32))


def _outproj(h, osb, om, w_out):
    t, d = h.shape
    width = osb.shape[1]
    tm = min(ROW_TILE, t)
    assert t % tm == 0 and w_out.shape == (2 * width, d)
    vmem = 2 * 2 * tm * d * 4 + 2 * 2 * tm * width * 2 + 2 * 2 * width * d * 2
    return pl.pallas_call(
        _outproj_kernel,
        grid=(t // tm,),
        in_specs=[
            pl.BlockSpec((tm, d), lambda i: (i, 0)),
            pl.BlockSpec((tm, width), lambda i: (i, 0)),
            pl.BlockSpec((tm, width), lambda i: (i, 0)),
            pl.BlockSpec((width, d), lambda i: (0, 0)),
            pl.BlockSpec((width, d), lambda i: (1, 0)),
        ],
        out_specs=pl.BlockSpec((tm, d), lambda i: (i, 0)),
        out_shape=jax.ShapeDtypeStruct((t, d), F32),
        compiler_params=_compiler_params(("parallel",), vmem),
        name="outproj",
    )(h, osb, om, w_out, w_out)


def _ple_kernel(h_ref, p_ref, gain_ref, wg_ref, wp_ref, fgain_ref, o_ref, *, final_norm):
    h = h_ref[...]
    n = (h * _rms_scale(h) * gain_ref[...]).astype(BF16)
    gate = jax.nn.sigmoid(jnp.dot(n, wg_ref[...], preferred_element_type=F32))
    proj = jnp.dot(p_ref[...].astype(BF16), wp_ref[...], preferred_element_type=F32)
    o = h + gate * proj
    if final_norm:
        o = o * _rms_scale(o) * fgain_ref[...]
    o_ref[...] = o


def _ple(h, p, gain, w_gate, w_proj, fgain, *, final_norm):
    t, d = h.shape
    pd = p.shape[1]
    tm = min(ROW_TILE, t)
    assert t % tm == 0
    vmem = 2 * 2 * tm * d * 4 + 2 * tm * pd * 4 + 2 * d * d * 2 + 2 * pd * d * 2 + 3 * tm * d * 4
    return pl.pallas_call(
        functools.partial(_ple_kernel, final_norm=final_norm),
        grid=(t // tm,),
        in_specs=[
            pl.BlockSpec((tm, d), lambda i: (i, 0)),
            pl.BlockSpec((tm, pd), lambda i: (i, 0)),
            pl.BlockSpec((1, d), lambda i: (0, 0)),
            pl.BlockSpec((d, d), lambda i: (0, 0)),
            pl.BlockSpec((pd, d), lambda i: (0, 0)),
            pl.BlockSpec((1, d), lambda i: (0, 0)),
        ],
        out_specs=pl.BlockSpec((tm, d), lambda i: (i, 0)),
        out_shape=jax.ShapeDtypeStruct((t, d), F32),
        compiler_params=_compiler_params(("parallel",), vmem),
        name="ple",
    )(h, p, gain.reshape(1, d), w_gate, w_proj, fgain.reshape(1, d))


def _gmlp_weights(ws, bs, seq_len):
    length = min(seq_len, MLP_CHUNK)
    assert MLP_CHUNK % length == 0
    reps = MLP_CHUNK // length
    if reps > 1:
        eye = jnp.eye(reps, dtype=ws.dtype)
        ws = jnp.einsum("ab,gij->gaibj", eye, ws[:, :length, :length]).reshape(
            ws.shape[0], MLP_CHUNK, MLP_CHUNK)
        bs = jnp.tile(bs[:, :length], (1, reps))
    return ws, jnp.transpose(bs)


def _layer(x, p, w, *, seq_len, final_gain, cache=None):
    rows = x.shape[0]
    h = _ffn(x, w["ffn1_norm"], w["ffn1_wg"], w["ffn1_wu"], w["ffn1_wd"])
    ws, bst = _gmlp_weights(w["gmlp_ws"], w["gmlp_bs"], seq_len)
    outs = _inproj(h, w["mix_norm"], w["w_in"], w["gmlp_v_norm"], ws, bst,
                   emit_zn=cache is not None)
    q, k, v, kb, vb, om = outs[:6]
    if cache is None:
        osb = _attn_prompt(q, kb, vb)
        zn = None
    else:
        osb = _attn_sample(q, kb, vb, cache[0], cache[1], rows // seq_len)
        zn = outs[6]
    h = _outproj(h, osb, om, w["w_out"])
    h = _ffn(h, w["ffn2_norm"], w["ffn2_wg"], w["ffn2_wu"], w["ffn2_wd"])
    h = _ple(h, p, w["ple_norm"], w["ple_w_gate"], w["ple_w_proj"],
             final_gain if final_gain is not None else w["ple_norm"],
             final_norm=final_gain is not None)
    return h, k, v, zn


_MATMUL_WEIGHTS = ("ffn1_wg", "ffn1_wu", "ffn1_wd", "w_in", "w_out",
                   "ffn2_wg", "ffn2_wu", "ffn2_wd", "ple_w_gate", "ple_w_proj")


def kernel(x_prompt, x_sample, cache_k, cache_v, p_prompt, p_sample, ffn1_norm, ffn1_wg, ffn1_wu, ffn1_wd, mix_norm, w_in, gmlp_v_norm, gmlp_ws, gmlp_bs, w_out, ffn2_norm, ffn2_wg, ffn2_wu, ffn2_wd, ple_norm, ple_w_gate, ple_w_proj, final_norm):
    weights = dict(ffn1_norm=ffn1_norm, ffn1_wg=ffn1_wg, ffn1_wu=ffn1_wu, ffn1_wd=ffn1_wd,
                   mix_norm=mix_norm, w_in=w_in, gmlp_v_norm=gmlp_v_norm, gmlp_ws=gmlp_ws,
                   gmlp_bs=gmlp_bs, w_out=w_out, ffn2_norm=ffn2_norm, ffn2_wg=ffn2_wg,
                   ffn2_wu=ffn2_wu, ffn2_wd=ffn2_wd, ple_norm=ple_norm,
                   ple_w_gate=ple_w_gate, ple_w_proj=ple_w_proj)
    depth = w_in.shape[0]
    bp, sp, d = x_prompt.shape
    bs, ss, _ = x_sample.shape
    assert bp == 1, "prompt attention handles one sequence"
    heads = cache_k.shape[3]
    width = heads * SB_HEAD_DIM

    hp = x_prompt.reshape(bp * sp, d)
    hs = x_sample.reshape(bs * ss, d)
    kp_l, vp_l, ks_l, vs_l, zs_l = [], [], [], [], []
    for i in range(depth):
        w = {name: (val[i].astype(BF16) if name in _MATMUL_WEIGHTS else val[i])
             for name, val in weights.items()}
        fgain = final_norm if i == depth - 1 else None
        hp, kp, vp, _ = _layer(hp, p_prompt[i].reshape(bp * sp, -1), w, seq_len=sp,
                               final_gain=fgain)
        cache = (cache_k[i].reshape(bs, -1, width), cache_v[i].reshape(bs, -1, width))
        hs, k_s, v_s, z_s = _layer(hs, p_sample[i].reshape(bs * ss, -1), w, seq_len=ss,
                                   final_gain=fgain, cache=cache)
        kp_l.append(kp.reshape(bp, sp, heads, SB_HEAD_DIM))
        vp_l.append(vp.reshape(bp, sp, heads, SB_HEAD_DIM))
        ks_l.append(k_s.reshape(bs, ss, heads, SB_HEAD_DIM))
        vs_l.append(v_s.reshape(bs, ss, heads, SB_HEAD_DIM))
        zs_l.append(z_s.reshape(bs, ss, -1, MLP_GROUP_DIM))
    return (hp.reshape(bp, sp, d), hs.reshape(bs, ss, d),
            jnp.stack(kp_l), jnp.stack(vp_l), jnp.stack(ks_l), jnp.stack(vs_l), jnp.stack(zs_l))
```

```python
import functools

import jax
import jax.numpy as jnp
from jax import lax
from jax.experimental import pallas as pl
from jax.experimental.pallas import tpu as pltpu

F32 = jnp.float32
BF16 = jnp.bfloat16

EPS = 1e-6
SB_HEAD_DIM = 128
MLP_GROUP_DIM = 128
MLP_CHUNK = 128

V7X_VMEM_BYTES = 64 * 1024 * 1024
VMEM_REQUEST_CAP = V7X_VMEM_BYTES - 6 * 1024 * 1024
LANES = 128

DEAD_LOG2 = 152.0
LOG2_E = 1.4426950408889634

ROW_TILE = 512
FFN_ROW_TILE = 1024
FFN_COL_TILE = 512
ATT_BLOCK = 256
ATT_HEADS_PER_STEP = 4


def _compiler_params(semantics, vmem_bytes):
    limit = min(int(vmem_bytes * 1.2) + (4 << 20), VMEM_REQUEST_CAP)
    return pltpu.CompilerParams(dimension_semantics=semantics, vmem_limit_bytes=limit)


def _rms_scale(x):
    return lax.rsqrt(jnp.mean(x * x, axis=-1, keepdims=True) + EPS)


def _ffn_kernel(x_ref, gain_ref, wg_ref, wu_ref, wd_ref, o_ref, n_ref, a0_ref, a1_ref, *, nf):
    j = pl.program_id(1)

    def up(a_ref):
        n = n_ref[...]
        g = jnp.dot(n, wg_ref[...], preferred_element_type=F32)
        u = jnp.dot(n, wu_ref[...], preferred_element_type=F32)
        a_ref[...] = (0.5 * g * jax.nn.sigmoid(g) * u).astype(BF16)

    def down(a_ref):
        o_ref[...] += jnp.dot(a_ref[...], wd_ref[...], preferred_element_type=F32)

    @pl.when(j == 0)
    def _():
        x = x_ref[...]
        n_ref[...] = (x * _rms_scale(x) * gain_ref[...]).astype(BF16)
        o_ref[...] = x
        up(a0_ref)

    middle = (j > 0) & (j < nf)

    @pl.when(middle & (j % 2 == 1))
    def _():
        down(a0_ref)
        up(a1_ref)

    @pl.when(middle & (j % 2 == 0))
    def _():
        down(a1_ref)
        up(a0_ref)

    @pl.when(j == nf)
    def _():
        down(a1_ref if (nf - 1) % 2 else a0_ref)


def _ffn(x, gain, wg, wu, wd):
    t, d = x.shape
    f = wg.shape[1]
    tm = min(FFN_ROW_TILE, t)
    tf = FFN_COL_TILE
    assert t % tm == 0 and f % tf == 0
    nf = f // tf
    vmem = (2 * 2 * tm * d * 4 + tm * d * 2 + 2 * 3 * d * tf * 2 + 2 * tm * tf * 2
            + 2 * tm * tf * 4)
    return pl.pallas_call(
        functools.partial(_ffn_kernel, nf=nf),
        grid=(t // tm, nf + 1),
        in_specs=[
            pl.BlockSpec((tm, d), lambda i, j: (i, 0)),
            pl.BlockSpec((1, d), lambda i, j: (0, 0)),
            pl.BlockSpec((d, tf), lambda i, j: (0, jnp.minimum(j, nf - 1))),
            pl.BlockSpec((d, tf), lambda i, j: (0, jnp.minimum(j, nf - 1))),
            pl.BlockSpec((tf, d), lambda i, j: (jnp.maximum(j - 1, 0), 0)),
        ],
        out_specs=pl.BlockSpec((tm, d), lambda i, j: (i, 0)),
        out_shape=jax.ShapeDtypeStruct((t, d), F32),
        scratch_shapes=[pltpu.VMEM((tm, d), BF16), pltpu.VMEM((tm, tf), BF16),
                        pltpu.VMEM((tm, tf), BF16)],
        compiler_params=_compiler_params(("parallel", "arbitrary"), vmem),
        name="ffn",
    )(x, gain.reshape(1, d), wg, wu, wd)


def _inproj_kernel(h_ref, gain_ref, w_ref, vgain_ref, ws_ref, bst_ref,
                   q_ref, k_ref, v_ref, kb_ref, vb_ref, om_ref, *rest, emit_zn):
    if emit_zn:
        zn_out_ref, n_ref, u_ref = rest
    else:
        n_ref, u_ref = rest
    j = pl.program_id(1)

    @pl.when(j == 0)
    def _():
        h = h_ref[...]
        n_ref[...] = (h * _rms_scale(h) * gain_ref[...]).astype(BF16)

    y = jnp.dot(n_ref[...], w_ref[...], preferred_element_type=F32)

    @pl.when(j == 0)
    def _():
        q_ref[...] = (y * (SB_HEAD_DIM ** -0.5 * LOG2_E)).astype(BF16)

    @pl.when(j == 1)
    def _():
        k_ref[...] = y
        kb_ref[...] = y.astype(BF16)

    @pl.when(j == 2)
    def _():
        v_ref[...] = y
        vb_ref[...] = y.astype(BF16)

    @pl.when(j == 3)
    def _():
        u_ref[...] = jax.nn.gelu(y, approximate=True)

    @pl.when(j == 4)
    def _():
        gz = jax.nn.gelu(y, approximate=True)
        zn = gz * _rms_scale(gz) * vgain_ref[...]
        if emit_zn:
            zn_out_ref[...] = zn
        znb = zn.astype(BF16)
        tm = znb.shape[0]
        groups = ws_ref.shape[0]
        row = lax.broadcasted_iota(jnp.int32, (MLP_CHUNK, MLP_CHUNK), 0)
        col = lax.broadcasted_iota(jnp.int32, (MLP_CHUNK, MLP_CHUNK), 1)
        lower = col <= row
        for g in range(groups):
            w = jnp.where(lower, ws_ref[g], 0.0).astype(BF16)
            bias = bst_ref[:, g:g + 1]
            cs = slice(g * MLP_GROUP_DIM, (g + 1) * MLP_GROUP_DIM)
            for c in range(tm // MLP_CHUNK):
                rs = slice(c * MLP_CHUNK, (c + 1) * MLP_CHUNK)
                mixed = jnp.dot(w, znb[rs, cs], preferred_element_type=F32) + bias
                om_ref[rs, cs] = (u_ref[rs, cs] * mixed).astype(BF16)


def _inproj(h, gain, w_in, vgain, ws, bst, *, emit_zn):
    t, d = h.shape
    width = w_in.shape[1] // 5
    tm = min(ROW_TILE, t)
    assert t % tm == 0 and tm % MLP_CHUNK == 0 and w_in.shape[1] == 5 * width
    groups = ws.shape[0]
    row_blk = lambda i, j: (i, 0)
    const2 = lambda i, j: (0, 0)
    out_shape = [
        jax.ShapeDtypeStruct((t, width), BF16),
        jax.ShapeDtypeStruct((t, width), F32),
        jax.ShapeDtypeStruct((t, width), F32),
        jax.ShapeDtypeStruct((t, width), BF16),
        jax.ShapeDtypeStruct((t, width), BF16),
        jax.ShapeDtypeStruct((t, width), BF16),
    ]
    if emit_zn:
        out_shape.append(jax.ShapeDtypeStruct((t, width), F32))
    out_specs = [pl.BlockSpec((tm, width), row_blk) for _ in out_shape]
    out_bytes = sum(tm * width * jnp.dtype(s.dtype).itemsize for s in out_shape)
    vmem = (2 * tm * d * 4 + 2 * d * width * 2 + 2 * out_bytes + tm * d * 2
            + tm * width * 4 + 4 * tm * width * 4)
    return pl.pallas_call(
        functools.partial(_inproj_kernel, emit_zn=emit_zn),
        grid=(t // tm, 5),
        in_specs=[
            pl.BlockSpec((tm, d), row_blk),
            pl.BlockSpec((1, d), const2),
            pl.BlockSpec((d, width), lambda i, j: (0, j)),
            pl.BlockSpec((1, width), const2),
            pl.BlockSpec((groups, MLP_CHUNK, MLP_CHUNK), lambda i, j: (0, 0, 0)),
            pl.BlockSpec((MLP_CHUNK, groups), const2),
        ],
        out_specs=out_specs,
        out_shape=out_shape,
        scratch_shapes=[pltpu.VMEM((tm, d), BF16), pltpu.VMEM((tm, width), F32)],
        compiler_params=_compiler_params(("parallel", "arbitrary"), vmem),
        name="inproj",
    )(h, gain.reshape(1, d), w_in, vgain.reshape(1, width), ws, bst)


def _init_tri(tri_ref):
    tk = tri_ref.shape[1]
    j = lax.broadcasted_iota(jnp.int32, (2 * tk, tk), 0)
    s = lax.broadcasted_iota(jnp.int32, (2 * tk, tk), 1)
    j = jnp.where(j >= tk, j - tk, j)
    tri_ref[...] = jnp.where(j > s, -1.0, 0.0).astype(BF16)


def _sb_span(q, k, v, valids, tri, carry):
    tb = tri.shape[1]
    u = lax.dot_general(q, k, (((1,), (1,)), ((), ())), preferred_element_type=F32)
    s = jnp.maximum(u, 0.0) + jnp.log2(1.0 + jnp.exp2(jnp.minimum(u, -u)))
    weights = [None] * len(valids)
    for b in reversed(range(len(valids))):
        cols = slice(b * tb, (b + 1) * tb)
        sb = s[:, cols]
        if valids[b] is not None:
            sb = jnp.where(valids[b], sb, 0.0)
        hi = sb.astype(BF16)
        lo = (sb - hi.astype(F32)).astype(BF16)
        newer = jnp.dot(jnp.concatenate([hi, lo], axis=1), tri, preferred_element_type=F32)
        log2_a = u[:, cols] - sb + newer
        total = jnp.sum(sb, axis=1, keepdims=True)
        if carry is None:
            carry = total
        else:
            log2_a = log2_a - carry
            carry = carry + total
        a = jnp.exp2(log2_a)
        if valids[b] is not None:
            a = jnp.where(valids[b], a, 0.0)
        weights[b] = a.astype(BF16)
    a = weights[0] if len(weights) == 1 else jnp.concatenate(weights, axis=1)
    return jnp.dot(a, v, preferred_element_type=F32), carry


def _head_rows(ref, h, rows):
    return ref.at[pl.ds(h * rows, rows)]


def _sb_older_blocks(q, load_kv, first_kb, tri, carry_ref, acc_ref):
    def cond(state):
        kb, low = state
        return (kb >= 0) & (low < DEAD_LOG2)

    def body(state):
        kb, _ = state
        k, v = load_kv(kb)
        o, carry = _sb_span(q, k, v, [None], tri, carry_ref[...])
        acc_ref[...] += o
        carry_ref[...] = carry
        return kb - 1, jnp.min(carry)

    return lax.while_loop(cond, body, (first_kb, jnp.min(carry_ref[...])))[1]


def _attn_prompt_kernel(q_ref, k_ref, v_ref, o_ref, tri_ref, carry_ref, acc_ref):
    i = pl.program_id(1)
    tb = q_ref.shape[0]
    heads = q_ref.shape[1] // SB_HEAD_DIM

    @pl.when((pl.program_id(0) == 0) & (i == 0))
    def _():
        _init_tri(tri_ref)

    tri = tri_ref[...]
    causal = (lax.broadcasted_iota(jnp.int32, (tb, tb), 1)
              < lax.broadcasted_iota(jnp.int32, (tb, tb), 0))
    head_cols = [slice(h * SB_HEAD_DIM, (h + 1) * SB_HEAD_DIM) for h in range(heads)]

    @pl.when(i == 0)
    def _():
        for hc in head_cols:
            o, _ = _sb_span(q_ref[:, hc], k_ref[0:tb, hc], v_ref[0:tb, hc], [causal], tri, None)
            o_ref[:, hc] = o.astype(o_ref.dtype)

    @pl.when(i > 0)
    def _():
        win = pl.ds(pl.multiple_of((i - 1) * tb, tb), 2 * tb)
        for h, hc in enumerate(head_cols):
            o, carry = _sb_span(q_ref[:, hc], k_ref[win, hc], v_ref[win, hc], [None, causal],
                                tri, None)
            _head_rows(acc_ref, h, tb)[...] = o
            _head_rows(carry_ref, h, tb)[...] = carry
        for h, hc in enumerate(head_cols):
            def load_kv(kb, hc=hc):
                rows = pl.ds(pl.multiple_of(kb * tb, tb), tb)
                return k_ref[rows, hc], v_ref[rows, hc]
            acc = _head_rows(acc_ref, h, tb)
            _sb_older_blocks(q_ref[:, hc], load_kv, i - 2, tri, _head_rows(carry_ref, h, tb), acc)
            o_ref[:, hc] = acc[...].astype(o_ref.dtype)


def _attn_prompt(q, k, v):
    s, width = q.shape
    hb = ATT_HEADS_PER_STEP
    tb = ATT_BLOCK
    assert s % tb == 0 and width % (hb * SB_HEAD_DIM) == 0
    wb = hb * SB_HEAD_DIM
    resident = pl.Buffered(1)
    vmem = 2 * s * wb * 2 + 2 * tb * tb * 2 + hb * 8 * tb * 2 * tb * 4
    return pl.pallas_call(
        _attn_prompt_kernel,
        grid=(width // wb, s // tb),
        in_specs=[
            pl.BlockSpec((tb, wb), lambda g, i: (i, g)),
            pl.BlockSpec((s, wb), lambda g, i: (0, g), pipeline_mode=resident),
            pl.BlockSpec((s, wb), lambda g, i: (0, g), pipeline_mode=resident),
        ],
        out_specs=pl.BlockSpec((tb, wb), lambda g, i: (i, g)),
        out_shape=jax.ShapeDtypeStruct((s, width), BF16),
        scratch_shapes=[
            pltpu.VMEM((2 * tb, tb), BF16),
            pltpu.VMEM((hb * tb, 1), F32),
            pltpu.VMEM((hb * tb, SB_HEAD_DIM), F32),
        ],
        compiler_params=_compiler_params(("arbitrary", "arbitrary"), vmem),
        name="attn_prompt",
    )(q, k, v)


def _attn_sample_kernel(q_ref, kn_ref, vn_ref, ck_ref, cv_ref, o_ref,
                        tri_ref, carry_ref, acc_ref, *, heads):
    tq = q_ref.shape[0]
    past = ck_ref.shape[1] // heads
    tb = tri_ref.shape[1]
    head_rows = tb - tq
    nfull = (past - head_rows) // tb
    rem = past - head_rows - nfull * tb

    @pl.when(pl.program_id(0) == 0)
    def _():
        _init_tri(tri_ref)

    tri = tri_ref[...]
    row = lax.broadcasted_iota(jnp.int32, (tq, tb), 0)
    col = lax.broadcasted_iota(jnp.int32, (tq, tb), 1)

    def cache_rows(ref, h, start, size):
        return ref[0, pl.ds(start * heads + h, size, stride=heads), :].astype(BF16)

    for h in range(heads):
        hc = slice(h * SB_HEAD_DIM, (h + 1) * SB_HEAD_DIM)
        k0 = jnp.concatenate([cache_rows(ck_ref, h, past - head_rows, head_rows), kn_ref[:, hc]],
                             axis=0)
        v0 = jnp.concatenate([cache_rows(cv_ref, h, past - head_rows, head_rows), vn_ref[:, hc]],
                             axis=0)
        o, carry = _sb_span(q_ref[:, hc], k0, v0, [col < row + head_rows], tri, None)
        _head_rows(acc_ref, h, tq)[...] = o
        _head_rows(carry_ref, h, tq)[...] = carry

    for h in range(heads):
        hc = slice(h * SB_HEAD_DIM, (h + 1) * SB_HEAD_DIM)
        q = q_ref[:, hc]
        acc = _head_rows(acc_ref, h, tq)
        carry_h = _head_rows(carry_ref, h, tq)

        def load_kv(kb, h=h):
            start = rem + kb * tb
            return cache_rows(ck_ref, h, start, tb), cache_rows(cv_ref, h, start, tb)

        low = _sb_older_blocks(q, load_kv, nfull - 1, tri, carry_h, acc)

        if rem:
            @pl.when(low < DEAD_LOG2)
            def _():
                o, _ = _sb_span(q, cache_rows(ck_ref, h, 0, tb), cache_rows(cv_ref, h, 0, tb),
                                [col < rem], tri, carry_h[...])
                acc[...] += o

        o_ref[:, hc] = acc[...].astype(o_ref.dtype)


def _attn_sample(q, kn, vn, cache_k, cache_v, batch, heads):
    rows, width = q.shape
    tq = rows // batch
    past = cache_k.shape[1] // heads
    tb = ATT_BLOCK
    assert tq < tb and past >= 2 * tb and tq % 16 == 0 and width == heads * SB_HEAD_DIM
    new_blk = lambda b: (b, 0)
    cache_blk = lambda b: (b, 0, 0)
    vmem = 2 * 2 * past * width * 4 + 2 * tb * tb * 2 + heads * 14 * tq * tb * 4
    return pl.pallas_call(
        functools.partial(_attn_sample_kernel, heads=heads),
        grid=(batch,),
        in_specs=[
            pl.BlockSpec((tq, width), new_blk),
            pl.BlockSpec((tq, width), new_blk),
            pl.BlockSpec((tq, width), new_blk),
            pl.BlockSpec((1, past * heads, SB_HEAD_DIM), cache_blk),
            pl.BlockSpec((1, past * heads, SB_HEAD_DIM), cache_blk),
        ],
        out_specs=pl.BlockSpec((tq, width), new_blk),
        out_shape=jax.ShapeDtypeStruct((rows, width), BF16),
        scratch_shapes=[
            pltpu.VMEM((2 * tb, tb), BF16),
            pltpu.VMEM((heads * tq, 1), F32),
            pltpu.VMEM((heads * tq, SB_HEAD_DIM), F32),
        ],
        compiler_params=_compiler_params(("arbitrary",), vmem),
        name="attn_sample",
    )(q, kn, vn, cache_k, cache_v)


def _outproj_kernel(h_ref, osb_ref, om_ref, wa_ref, wb_ref, o_ref):
    o_ref[...] = (h_ref[...]
                  + jnp.dot(osb_ref[...], wa_ref[...], preferred_element_type=F32)
                  + jnp.dot(om_ref[...], wb_ref[...], preferred_element_type=F32))


def _outproj(h, osb, om, w_out):
    t, d = h.shape
    width = osb.shape[1]
    tm = min(ROW_TILE, t)
    assert t % tm == 0 and w_out.shape == (2 * width, d)
    vmem = 2 * 2 * tm * d * 4 + 2 * 2 * tm * width * 2 + 2 * 2 * width * d * 2
    return pl.pallas_call(
        _outproj_kernel,
        grid=(t // tm,),
        in_specs=[
            pl.BlockSpec((tm, d), lambda i: (i, 0)),
            pl.BlockSpec((tm, width), lambda i: (i, 0)),
            pl.BlockSpec((tm, width), lambda i: (i, 0)),
            pl.BlockSpec((width, d), lambda i: (0, 0)),
            pl.BlockSpec((width, d), lambda i: (1, 0)),
        ],
        out_specs=pl.BlockSpec((tm, d), lambda i: (i, 0)),
        out_shape=jax.ShapeDtypeStruct((t, d), F32),
        compiler_params=_compiler_params(("parallel",), vmem),
        name="outproj",
    )(h, osb, om, w_out, w_out)


def _ple_kernel(h_ref, p_ref, gain_ref, wg_ref, wp_ref, fgain_ref, o_ref, *, final_norm):
    h = h_ref[...]
    n = (h * _rms_scale(h) * gain_ref[...]).astype(BF16)
    gate = jax.nn.sigmoid(jnp.dot(n, wg_ref[...], preferred_element_type=F32))
    proj = jnp.dot(p_ref[...].astype(BF16), wp_ref[...], preferred_element_type=F32)
    o = h + gate * proj
    if final_norm:
        o = o * _rms_scale(o) * fgain_ref[...]
    o_ref[...] = o


def _ple(h, p, gain, w_gate, w_proj, fgain, *, final_norm):
    t, d = h.shape
    pd = p.shape[1]
    tm = min(ROW_TILE, t)
    assert t % tm == 0
    vmem = 2 * 2 * tm * d * 4 + 2 * tm * pd * 4 + 2 * d * d * 2 + 2 * pd * d * 2 + 3 * tm * d * 4
    return pl.pallas_call(
        functools.partial(_ple_kernel, final_norm=final_norm),
        grid=(t // tm,),
        in_specs=[
            pl.BlockSpec((tm, d), lambda i: (i, 0)),
            pl.BlockSpec((tm, pd), lambda i: (i, 0)),
            pl.BlockSpec((1, d), lambda i: (0, 0)),
            pl.BlockSpec((d, d), lambda i: (0, 0)),
            pl.BlockSpec((pd, d), lambda i: (0, 0)),
            pl.BlockSpec((1, d), lambda i: (0, 0)),
        ],
        out_specs=pl.BlockSpec((tm, d), lambda i: (i, 0)),
        out_shape=jax.ShapeDtypeStruct((t, d), F32),
        compiler_params=_compiler_params(("parallel",), vmem),
        name="ple",
    )(h, p, gain.reshape(1, d), w_gate, w_proj, fgain.reshape(1, d))


def _gmlp_weights(ws, bs, seq_len):
    length = min(seq_len, MLP_CHUNK)
    assert MLP_CHUNK % length == 0
    reps = MLP_CHUNK // length
    if reps > 1:
        eye = jnp.eye(reps, dtype=ws.dtype)
        ws = jnp.einsum("ab,gij->gaibj", eye, ws[:, :length, :length]).reshape(
            ws.shape[0], MLP_CHUNK, MLP_CHUNK)
        bs = jnp.tile(bs[:, :length], (1, reps))
    return ws, jnp.transpose(bs)


def _layer(x, p, w, *, seq_len, final_gain, cache=None):
    rows = x.shape[0]
    h = _ffn(x, w["ffn1_norm"], w["ffn1_wg"], w["ffn1_wu"], w["ffn1_wd"])
    ws, bst = _gmlp_weights(w["gmlp_ws"], w["gmlp_bs"], seq_len)
    outs = _inproj(h, w["mix_norm"], w["w_in"], w["gmlp_v_norm"], ws, bst,
                   emit_zn=cache is not None)
    q, k, v, kb, vb, om = outs[:6]
    if cache is None:
        osb = _attn_prompt(q, kb, vb)
        zn = None
    else:
        osb = _attn_sample(q, kb, vb, cache[0], cache[1], rows // seq_len,
                           q.shape[1] // SB_HEAD_DIM)
        zn = outs[6]
    h = _outproj(h, osb, om, w["w_out"])
    h = _ffn(h, w["ffn2_norm"], w["ffn2_wg"], w["ffn2_wu"], w["ffn2_wd"])
    h = _ple(h, p, w["ple_norm"], w["ple_w_gate"], w["ple_w_proj"],
             final_gain if final_gain is not None else w["ple_norm"],
             final_norm=final_gain is not None)
    return h, k, v, zn


_MATMUL_WEIGHTS = ("ffn1_wg", "ffn1_wu", "ffn1_wd", "w_in", "w_out",
                   "ffn2_wg", "ffn2_wu", "ffn2_wd", "ple_w_gate", "ple_w_proj")


def kernel(x_prompt, x_sample, cache_k, cache_v, p_prompt, p_sample, ffn1_norm, ffn1_wg, ffn1_wu, ffn1_wd, mix_norm, w_in, gmlp_v_norm, gmlp_ws, gmlp_bs, w_out, ffn2_norm, ffn2_wg, ffn2_wu, ffn2_wd, ple_norm, ple_w_gate, ple_w_proj, final_norm):
    weights = dict(ffn1_norm=ffn1_norm, ffn1_wg=ffn1_wg, ffn1_wu=ffn1_wu, ffn1_wd=ffn1_wd,
                   mix_norm=mix_norm, w_in=w_in, gmlp_v_norm=gmlp_v_norm, gmlp_ws=gmlp_ws,
                   gmlp_bs=gmlp_bs, w_out=w_out, ffn2_norm=ffn2_norm, ffn2_wg=ffn2_wg,
                   ffn2_wu=ffn2_wu, ffn2_wd=ffn2_wd, ple_norm=ple_norm,
                   ple_w_gate=ple_w_gate, ple_w_proj=ple_w_proj)
    depth = w_in.shape[0]
    bp, sp, d = x_prompt.shape
    bs, ss, _ = x_sample.shape
    assert bp == 1, "prompt attention handles one sequence"
    heads = cache_k.shape[3]
    width = heads * SB_HEAD_DIM

    hp = x_prompt.reshape(bp * sp, d)
    hs = x_sample.reshape(bs * ss, d)
    kp_l, vp_l, ks_l, vs_l, zs_l = [], [], [], [], []
    for i in range(depth):
        w = {name: (val[i].astype(BF16) if name in _MATMUL_WEIGHTS else val[i])
             for name, val in weights.items()}
        fgain = final_norm if i == depth - 1 else None
        hp, kp, vp, _ = _layer(hp, p_prompt[i].reshape(bp * sp, -1), w, seq_len=sp,
                               final_gain=fgain)
        cache = (cache_k[i].reshape(bs, -1, SB_HEAD_DIM), cache_v[i].reshape(bs, -1, SB_HEAD_DIM))
        hs, k_s, v_s, z_s = _layer(hs, p_sample[i].reshape(bs * ss, -1), w, seq_len=ss,
                                   final_gain=fgain, cache=cache)
        kp_l.append(kp.reshape(bp, sp, heads, SB_HEAD_DIM))
        vp_l.append(vp.reshape(bp, sp, heads, SB_HEAD_DIM))
        ks_l.append(k_s.reshape(bs, ss, heads, SB_HEAD_DIM))
        vs_l.append(v_s.reshape(bs, ss, heads, SB_HEAD_DIM))
        zs_l.append(z_s.reshape(bs, ss, -1, MLP_GROUP_DIM))
    return (hp.reshape(bp, sp, d), hs.reshape(bs, ss, d),
            jnp.stack(kp_l), jnp.stack(vp_l), jnp.stack(ks_l), jnp.stack(vs_l), jnp.stack(zs_l))
```

```python
import functools

import jax
import jax.numpy as jnp
from jax import lax
from jax.experimental import pallas as pl
from jax.experimental.pallas import tpu as pltpu

F32 = jnp.float32
BF16 = jnp.bfloat16

EPS = 1e-6
SB_HEAD_DIM = 128
MLP_GROUP_DIM = 128
MLP_CHUNK = 128

V7X_VMEM_BYTES = 64 * 1024 * 1024
VMEM_REQUEST_CAP = V7X_VMEM_BYTES - 6 * 1024 * 1024
LANES = 128

DEAD_LOG2 = 152.0
LOG2_E = 1.4426950408889634

ROW_TILE = 512
FFN_ROW_TILE = 1024
FFN_COL_TILE = 512
FFN_CAST_COL_TILE = 256
ATT_BLOCK = 256
ATT_HEADS_PER_STEP = 4


def _compiler_params(semantics, vmem_bytes):
    limit = min(int(vmem_bytes * 1.2) + (4 << 20), VMEM_REQUEST_CAP)
    return pltpu.CompilerParams(dimension_semantics=semantics, vmem_limit_bytes=limit)


def _rms_scale(x):
    return lax.rsqrt(jnp.mean(x * x, axis=-1, keepdims=True) + EPS)


def _bf16_weight(w_ref, cast_ref):
    if cast_ref is None:
        return w_ref[...]
    w = w_ref[...].astype(BF16)
    cast_ref[...] = w
    return w


def _ffn_kernel(x_ref, gain_ref, wg_ref, wu_ref, wd_ref, o_ref, *rest, nf, cast_weights):
    if cast_weights:
        wg_cast, wu_cast, wd_cast, n_ref, a0_ref, a1_ref = rest
    else:
        n_ref, a0_ref, a1_ref = rest
        wg_cast = wu_cast = wd_cast = None
    j = pl.program_id(1)

    def up(a_ref):
        n = n_ref[...]
        g = jnp.dot(n, _bf16_weight(wg_ref, wg_cast), preferred_element_type=F32)
        u = jnp.dot(n, _bf16_weight(wu_ref, wu_cast), preferred_element_type=F32)
        a_ref[...] = (0.5 * g * jax.nn.sigmoid(g) * u).astype(BF16)

    def down(a_ref):
        o_ref[...] += jnp.dot(a_ref[...], _bf16_weight(wd_ref, wd_cast),
                              preferred_element_type=F32)

    @pl.when(j == 0)
    def _():
        x = x_ref[...]
        n_ref[...] = (x * _rms_scale(x) * gain_ref[...]).astype(BF16)
        o_ref[...] = x
        up(a0_ref)

    middle = (j > 0) & (j < nf)

    @pl.when(middle & (j % 2 == 1))
    def _():
        down(a0_ref)
        up(a1_ref)

    @pl.when(middle & (j % 2 == 0))
    def _():
        down(a1_ref)
        up(a0_ref)

    @pl.when(j == nf)
    def _():
        down(a1_ref if (nf - 1) % 2 else a0_ref)


def _ffn(x, gain, wg, wu, wd):
    t, d = x.shape
    f = wg.shape[1]
    cast_weights = wg.dtype != BF16
    tm = min(FFN_ROW_TILE, t)
    tf = FFN_CAST_COL_TILE if cast_weights else FFN_COL_TILE
    assert t % tm == 0 and f % tf == 0
    assert not cast_weights or t == tm
    nf = f // tf
    wbytes = jnp.dtype(wg.dtype).itemsize
    x_bufs = 1 if cast_weights else 2
    vmem = ((x_bufs + 2) * tm * d * 4 + tm * d * 2 + 2 * 3 * d * tf * wbytes
            + 2 * tm * tf * 2 + 2 * tm * tf * 4)
    up_blk = lambda i, j: (0, jnp.minimum(j, nf - 1))
    down_blk = lambda i, j: (jnp.maximum(j - 1, 0), 0)
    row_blk = lambda i, j: (i, 0)
    out_specs = [pl.BlockSpec((tm, d), row_blk)]
    out_shape = [jax.ShapeDtypeStruct((t, d), F32)]
    if cast_weights:
        vmem += 2 * 3 * d * tf * 2
        out_specs += [pl.BlockSpec((d, tf), up_blk), pl.BlockSpec((d, tf), up_blk),
                      pl.BlockSpec((tf, d), down_blk)]
        out_shape += [jax.ShapeDtypeStruct(w.shape, BF16) for w in (wg, wu, wd)]
    x_spec = (pl.BlockSpec((tm, d), row_blk, pipeline_mode=pl.Buffered(1)) if cast_weights
              else pl.BlockSpec((tm, d), row_blk))
    outs = pl.pallas_call(
        functools.partial(_ffn_kernel, nf=nf, cast_weights=cast_weights),
        grid=(t // tm, nf + 1),
        in_specs=[
            x_spec,
            pl.BlockSpec((1, d), lambda i, j: (0, 0)),
            pl.BlockSpec((d, tf), up_blk),
            pl.BlockSpec((d, tf), up_blk),
            pl.BlockSpec((tf, d), down_blk),
        ],
        out_specs=out_specs,
        out_shape=out_shape,
        scratch_shapes=[pltpu.VMEM((tm, d), BF16), pltpu.VMEM((tm, tf), BF16),
                        pltpu.VMEM((tm, tf), BF16)],
        compiler_params=_compiler_params(("parallel", "arbitrary"), vmem),
        name="ffn",
    )(x, gain.reshape(1, d), wg, wu, wd)
    return (outs[0], tuple(outs[1:])) if cast_weights else (outs[0], (wg, wu, wd))


INPROJ_GROUP_ORDER = (3, 4, 0, 1, 2)


def _inproj_kernel(h_ref, gain_ref, w_ref, vgain_ref, ws_ref, bst_ref,
                   q_ref, k_ref, v_ref, kb_ref, vb_ref, om_ref, *rest, emit_zn):
    if emit_zn:
        zn_out_ref, n_ref, u_ref = rest
    else:
        n_ref, u_ref = rest
    j = pl.program_id(1)

    def finish_u(y):
        u_ref[...] = jax.nn.gelu(y, approximate=True)

    def finish_z(y):
        gz = jax.nn.gelu(y, approximate=True)
        zn = gz * _rms_scale(gz) * vgain_ref[...]
        if emit_zn:
            zn_out_ref[...] = zn
        znb = zn.astype(BF16)
        tm = znb.shape[0]
        groups = ws_ref.shape[0]
        row = lax.broadcasted_iota(jnp.int32, (MLP_CHUNK, MLP_CHUNK), 0)
        col = lax.broadcasted_iota(jnp.int32, (MLP_CHUNK, MLP_CHUNK), 1)
        lower = col <= row
        for g in range(groups):
            w = jnp.where(lower, ws_ref[g], 0.0).astype(BF16)
            bias = bst_ref[:, g:g + 1]
            cs = slice(g * MLP_GROUP_DIM, (g + 1) * MLP_GROUP_DIM)
            for c in range(tm // MLP_CHUNK):
                rs = slice(c * MLP_CHUNK, (c + 1) * MLP_CHUNK)
                mixed = jnp.dot(w, znb[rs, cs], preferred_element_type=F32) + bias
                om_ref[rs, cs] = (u_ref[rs, cs] * mixed).astype(BF16)

    def finish_q(y):
        q_ref[...] = (y * (SB_HEAD_DIM ** -0.5 * LOG2_E)).astype(BF16)

    def finish_k(y):
        k_ref[...] = y
        kb_ref[...] = y.astype(BF16)

    def finish_v(y):
        v_ref[...] = y
        vb_ref[...] = y.astype(BF16)

    @pl.when(j == 0)
    def _():
        h = h_ref[...]
        n_ref[...] = (h * _rms_scale(h) * gain_ref[...]).astype(BF16)

    for step, finish in enumerate((finish_u, finish_z, finish_q, finish_k, finish_v)):
        @pl.when(j == step)
        def _(finish=finish):
            finish(jnp.dot(n_ref[...], w_ref[...], preferred_element_type=F32))


def _inproj(h, gain, w_in, vgain, ws, bst, *, emit_zn):
    t, d = h.shape
    width = w_in.shape[1] // 5
    tm = min(ROW_TILE, t)
    assert t % tm == 0 and tm % MLP_CHUNK == 0 and w_in.shape[1] == 5 * width
    groups = ws.shape[0]
    row_blk = lambda i, j: (i, 0)
    const2 = lambda i, j: (0, 0)
    out_shape = [
        jax.ShapeDtypeStruct((t, width), BF16),
        jax.ShapeDtypeStruct((t, width), F32),
        jax.ShapeDtypeStruct((t, width), F32),
        jax.ShapeDtypeStruct((t, width), BF16),
        jax.ShapeDtypeStruct((t, width), BF16),
        jax.ShapeDtypeStruct((t, width), BF16),
    ]
    if emit_zn:
        out_shape.append(jax.ShapeDtypeStruct((t, width), F32))
    out_specs = [pl.BlockSpec((tm, width), row_blk) for _ in out_shape]
    out_bytes = sum(tm * width * jnp.dtype(s.dtype).itemsize for s in out_shape)
    vmem = (2 * tm * d * 4 + 2 * d * width * 2 + 2 * out_bytes + tm * d * 2
            + tm * width * 4 + 4 * tm * width * 4)

    def w_blk(i, j):
        group = INPROJ_GROUP_ORDER[-1]
        for step, g in enumerate(INPROJ_GROUP_ORDER[:-1]):
            group = jnp.where(j == step, g, group)
        return (0, group)

    return pl.pallas_call(
        functools.partial(_inproj_kernel, emit_zn=emit_zn),
        grid=(t // tm, len(INPROJ_GROUP_ORDER)),
        in_specs=[
            pl.BlockSpec((tm, d), row_blk),
            pl.BlockSpec((1, d), const2),
            pl.BlockSpec((d, width), w_blk),
            pl.BlockSpec((1, width), const2),
            pl.BlockSpec((groups, MLP_CHUNK, MLP_CHUNK), lambda i, j: (0, 0, 0)),
            pl.BlockSpec((MLP_CHUNK, groups), const2),
        ],
        out_specs=out_specs,
        out_shape=out_shape,
        scratch_shapes=[pltpu.VMEM((tm, d), BF16), pltpu.VMEM((tm, width), F32)],
        compiler_params=_compiler_params(("parallel", "arbitrary"), vmem),
        name="inproj",
    )(h, gain.reshape(1, d), w_in, vgain.reshape(1, width), ws, bst)


def _init_tri(tri_ref):
    j = lax.broadcasted_iota(jnp.int32, tri_ref.shape, 0)
    s = lax.broadcasted_iota(jnp.int32, tri_ref.shape, 1)
    tri_ref[...] = jnp.where(j > s, -1.0, 0.0).astype(BF16)


def _sb_span(q, k, v, valids, tri, carry):
    tb = tri.shape[1]
    u = lax.dot_general(q, k, (((1,), (1,)), ((), ())), preferred_element_type=F32)
    s = jnp.maximum(u, 0.0) + jnp.log2(1.0 + jnp.exp2(jnp.minimum(u, -u)))
    weights = [None] * len(valids)
    for b in reversed(range(len(valids))):
        cols = slice(b * tb, (b + 1) * tb)
        sb = s[:, cols]
        if valids[b] is not None:
            sb = jnp.where(valids[b], sb, 0.0)
        newer = jnp.dot(sb.astype(BF16), tri, preferred_element_type=F32)
        log2_a = u[:, cols] - sb + newer
        total = jnp.sum(sb, axis=1, keepdims=True)
        if carry is None:
            carry = total
        else:
            log2_a = log2_a - carry
            carry = carry + total
        a = jnp.exp2(log2_a)
        if valids[b] is not None:
            a = jnp.where(valids[b], a, 0.0)
        weights[b] = a.astype(BF16)
    a = weights[0] if len(weights) == 1 else jnp.concatenate(weights, axis=1)
    return jnp.dot(a, v, preferred_element_type=F32), carry


def _head_rows(ref, h, rows):
    return ref.at[pl.ds(h * rows, rows)]


def _sb_older_blocks(q, load_kv, first_kb, tri, carry_ref, acc_ref):
    def cond(state):
        kb, low = state
        return (kb >= 0) & (low < DEAD_LOG2)

    def body(state):
        kb, _ = state
        k, v = load_kv(kb)
        o, carry = _sb_span(q, k, v, [None], tri, carry_ref[...])
        acc_ref[...] += o
        carry_ref[...] = carry
        return kb - 1, jnp.min(carry)

    return lax.while_loop(cond, body, (first_kb, jnp.min(carry_ref[...])))[1]


def _attn_prompt_kernel(q_ref, k_ref, v_ref, o_ref, tri_ref, carry_ref, acc_ref):
    i = pl.program_id(1)
    tb = q_ref.shape[0]
    heads = q_ref.shape[1] // SB_HEAD_DIM

    @pl.when((pl.program_id(0) == 0) & (i == 0))
    def _():
        _init_tri(tri_ref)

    tri = tri_ref[...]
    causal = (lax.broadcasted_iota(jnp.int32, (tb, tb), 1)
              < lax.broadcasted_iota(jnp.int32, (tb, tb), 0))
    head_cols = [slice(h * SB_HEAD_DIM, (h + 1) * SB_HEAD_DIM) for h in range(heads)]

    @pl.when(i == 0)
    def _():
        for hc in head_cols:
            o, _ = _sb_span(q_ref[:, hc], k_ref[0:tb, hc], v_ref[0:tb, hc], [causal], tri, None)
            o_ref[:, hc] = o.astype(o_ref.dtype)

    @pl.when(i > 0)
    def _():
        win = pl.ds(pl.multiple_of((i - 1) * tb, tb), 2 * tb)
        for h, hc in enumerate(head_cols):
            o, carry = _sb_span(q_ref[:, hc], k_ref[win, hc], v_ref[win, hc], [None, causal],
                                tri, None)
            _head_rows(acc_ref, h, tb)[...] = o
            _head_rows(carry_ref, h, tb)[...] = carry
        for h, hc in enumerate(head_cols):
            def load_kv(kb, hc=hc):
                rows = pl.ds(pl.multiple_of(kb * tb, tb), tb)
                return k_ref[rows, hc], v_ref[rows, hc]
            acc = _head_rows(acc_ref, h, tb)
            _sb_older_blocks(q_ref[:, hc], load_kv, i - 2, tri, _head_rows(carry_ref, h, tb), acc)
            o_ref[:, hc] = acc[...].astype(o_ref.dtype)


def _attn_prompt(q, k, v):
    s, width = q.shape
    hb = ATT_HEADS_PER_STEP
    tb = ATT_BLOCK
    assert s % tb == 0 and width % (hb * SB_HEAD_DIM) == 0
    wb = hb * SB_HEAD_DIM
    resident = pl.Buffered(1)
    vmem = 2 * s * wb * 2 + 2 * tb * tb * 2 + hb * 8 * tb * 2 * tb * 4
    return pl.pallas_call(
        _attn_prompt_kernel,
        grid=(width // wb, s // tb),
        in_specs=[
            pl.BlockSpec((tb, wb), lambda g, i: (i, g)),
            pl.BlockSpec((s, wb), lambda g, i: (0, g), pipeline_mode=resident),
            pl.BlockSpec((s, wb), lambda g, i: (0, g), pipeline_mode=resident),
        ],
        out_specs=pl.BlockSpec((tb, wb), lambda g, i: (i, g)),
        out_shape=jax.ShapeDtypeStruct((s, width), BF16),
        scratch_shapes=[
            pltpu.VMEM((tb, tb), BF16),
            pltpu.VMEM((hb * tb, 1), F32),
            pltpu.VMEM((hb * tb, SB_HEAD_DIM), F32),
        ],
        compiler_params=_compiler_params(("arbitrary", "arbitrary"), vmem),
        name="attn_prompt",
    )(q, k, v)


def _attn_sample_kernel(q_ref, kn_ref, vn_ref, ck_ref, cv_ref, o_ref,
                        tri_ref, carry_ref, acc_ref, *, heads):
    tq = q_ref.shape[0]
    past = ck_ref.shape[1] // heads
    tb = tri_ref.shape[1]
    head_rows = tb - tq
    nfull = (past - head_rows) // tb
    rem = past - head_rows - nfull * tb

    @pl.when(pl.program_id(0) == 0)
    def _():
        _init_tri(tri_ref)

    tri = tri_ref[...]
    row = lax.broadcasted_iota(jnp.int32, (tq, tb), 0)
    col = lax.broadcasted_iota(jnp.int32, (tq, tb), 1)

    def cache_rows(ref, h, start, size):
        return ref[0, pl.ds(start * heads + h, size, stride=heads), :].astype(BF16)

    for h in range(heads):
        hc = slice(h * SB_HEAD_DIM, (h + 1) * SB_HEAD_DIM)
        k0 = jnp.concatenate([cache_rows(ck_ref, h, past - head_rows, head_rows), kn_ref[:, hc]],
                             axis=0)
        v0 = jnp.concatenate([cache_rows(cv_ref, h, past - head_rows, head_rows), vn_ref[:, hc]],
                             axis=0)
        o, carry = _sb_span(q_ref[:, hc], k0, v0, [col < row + head_rows], tri, None)
        _head_rows(acc_ref, h, tq)[...] = o
        _head_rows(carry_ref, h, tq)[...] = carry

    for h in range(heads):
        hc = slice(h * SB_HEAD_DIM, (h + 1) * SB_HEAD_DIM)
        q = q_ref[:, hc]
        acc = _head_rows(acc_ref, h, tq)
        carry_h = _head_rows(carry_ref, h, tq)

        def load_kv(kb, h=h):
            start = rem + kb * tb
            return cache_rows(ck_ref, h, start, tb), cache_rows(cv_ref, h, start, tb)

        low = _sb_older_blocks(q, load_kv, nfull - 1, tri, carry_h, acc)

        if rem:
            @pl.when(low < DEAD_LOG2)
            def _():
                o, _ = _sb_span(q, cache_rows(ck_ref, h, 0, tb), cache_rows(cv_ref, h, 0, tb),
                                [col < rem], tri, carry_h[...])
                acc[...] += o

        o_ref[:, hc] = acc[...].astype(o_ref.dtype)


def _attn_sample(q, kn, vn, cache_k, cache_v, batch, heads):
    rows, width = q.shape
    tq = rows // batch
    past = cache_k.shape[1] // heads
    tb = ATT_BLOCK
    assert tq < tb and past >= 2 * tb and tq % 16 == 0 and width == heads * SB_HEAD_DIM
    new_blk = lambda b: (b, 0)
    cache_blk = lambda b: (b, 0, 0)
    vmem = 2 * 2 * past * width * 4 + 2 * tb * tb * 2 + heads * 14 * tq * tb * 4
    return pl.pallas_call(
        functools.partial(_attn_sample_kernel, heads=heads),
        grid=(batch,),
        in_specs=[
            pl.BlockSpec((tq, width), new_blk),
            pl.BlockSpec((tq, width), new_blk),
            pl.BlockSpec((tq, width), new_blk),
            pl.BlockSpec((1, past * heads, SB_HEAD_DIM), cache_blk),
            pl.BlockSpec((1, past * heads, SB_HEAD_DIM), cache_blk),
        ],
        out_specs=pl.BlockSpec((tq, width), new_blk),
        out_shape=jax.ShapeDtypeStruct((rows, width), BF16),
        scratch_shapes=[
            pltpu.VMEM((tb, tb), BF16),
            pltpu.VMEM((heads * tq, 1), F32),
            pltpu.VMEM((heads * tq, SB_HEAD_DIM), F32),
        ],
        compiler_params=_compiler_params(("arbitrary",), vmem),
        name="attn_sample",
    )(q, kn, vn, cache_k, cache_v)


def _outproj_kernel(h_ref, osb_ref, om_ref, wa_ref, wb_ref, o_ref):
    o_ref[...] = (h_ref[...]
                  + jnp.dot(osb_ref[...], wa_ref[...], preferred_element_type=F32)
                  + jnp.dot(om_ref[...], wb_ref[...], preferred_element_type=F32))


def _outproj(h, osb, om, w_out):
    t, d = h.shape
    width = osb.shape[1]
    tm = min(ROW_TILE, t)
    assert t % tm == 0 and w_out.shape == (2 * width, d)
    vmem = 2 * 2 * tm * d * 4 + 2 * 2 * tm * width * 2 + 2 * 2 * width * d * 2
    return pl.pallas_call(
        _outproj_kernel,
        grid=(t // tm,),
        in_specs=[
            pl.BlockSpec((tm, d), lambda i: (i, 0)),
            pl.BlockSpec((tm, width), lambda i: (i, 0)),
            pl.BlockSpec((tm, width), lambda i: (i, 0)),
            pl.BlockSpec((width, d), lambda i: (0, 0)),
            pl.BlockSpec((width, d), lambda i: (1, 0)),
        ],
        out_specs=pl.BlockSpec((tm, d), lambda i: (i, 0)),
        out_shape=jax.ShapeDtypeStruct((t, d), F32),
        compiler_params=_compiler_params(("parallel",), vmem),
        name="outproj",
    )(h, osb, om, w_out, w_out)


def _ple_kernel(h_ref, p_ref, gain_ref, wg_ref, wp_ref, fgain_ref, o_ref, *, final_norm):
    h = h_ref[...]
    n = (h * _rms_scale(h) * gain_ref[...]).astype(BF16)
    gate = jax.nn.sigmoid(jnp.dot(n, wg_ref[...], preferred_element_type=F32))
    proj = jnp.dot(p_ref[...].astype(BF16), wp_ref[...], preferred_element_type=F32)
    o = h + gate * proj
    if final_norm:
        o = o * _rms_scale(o) * fgain_ref[...]
    o_ref[...] = o


def _ple(h, p, gain, w_gate, w_proj, fgain, *, final_norm):
    t, d = h.shape
    pd = p.shape[1]
    tm = min(ROW_TILE, t)
    assert t % tm == 0
    vmem = 2 * 2 * tm * d * 4 + 2 * tm * pd * 4 + 2 * d * d * 2 + 2 * pd * d * 2 + 3 * tm * d * 4
    return pl.pallas_call(
        functools.partial(_ple_kernel, final_norm=final_norm),
        grid=(t // tm,),
        in_specs=[
            pl.BlockSpec((tm, d), lambda i: (i, 0)),
            pl.BlockSpec((tm, pd), lambda i: (i, 0)),
            pl.BlockSpec((1, d), lambda i: (0, 0)),
            pl.BlockSpec((d, d), lambda i: (0, 0)),
            pl.BlockSpec((pd, d), lambda i: (0, 0)),
            pl.BlockSpec((1, d), lambda i: (0, 0)),
        ],
        out_specs=pl.BlockSpec((tm, d), lambda i: (i, 0)),
        out_shape=jax.ShapeDtypeStruct((t, d), F32),
        compiler_params=_compiler_params(("parallel",), vmem),
        name="ple",
    )(h, p, gain.reshape(1, d), w_gate, w_proj, fgain.reshape(1, d))


def _gmlp_weights(ws, bs, seq_len):
    length = min(seq_len, MLP_CHUNK)
    assert MLP_CHUNK % length == 0
    reps = MLP_CHUNK // length
    if reps > 1:
        eye = jnp.eye(reps, dtype=ws.dtype)
        ws = jnp.einsum("ab,gij->gaibj", eye, ws[:, :length, :length]).reshape(
            ws.shape[0], MLP_CHUNK, MLP_CHUNK)
        bs = jnp.tile(bs[:, :length], (1, reps))
    return ws, jnp.transpose(bs)


def _layer(x, p, w, *, seq_len, final_gain, cache=None):
    rows = x.shape[0]
    w = dict(w)
    h, (w["ffn1_wg"], w["ffn1_wu"], w["ffn1_wd"]) = _ffn(
        x, w["ffn1_norm"], w["ffn1_wg"], w["ffn1_wu"], w["ffn1_wd"])
    ws, bst = _gmlp_weights(w["gmlp_ws"], w["gmlp_bs"], seq_len)
    outs = _inproj(h, w["mix_norm"], w["w_in"], w["gmlp_v_norm"], ws, bst,
                   emit_zn=cache is not None)
    q, k, v, kb, vb, om = outs[:6]
    if cache is None:
        osb = _attn_prompt(q, kb, vb)
        zn = None
    else:
        osb = _attn_sample(q, kb, vb, cache[0], cache[1], rows // seq_len,
                           q.shape[1] // SB_HEAD_DIM)
        zn = outs[6]
    h = _outproj(h, osb, om, w["w_out"])
    h, (w["ffn2_wg"], w["ffn2_wu"], w["ffn2_wd"]) = _ffn(
        h, w["ffn2_norm"], w["ffn2_wg"], w["ffn2_wu"], w["ffn2_wd"])
    h = _ple(h, p, w["ple_norm"], w["ple_w_gate"], w["ple_w_proj"],
             final_gain if final_gain is not None else w["ple_norm"],
             final_norm=final_gain is not None)
    return h, k, v, zn, w


_PRECAST_WEIGHTS = ("w_in", "w_out", "ple_w_gate", "ple_w_proj")


def kernel(x_prompt, x_sample, cache_k, cache_v, p_prompt, p_sample, ffn1_norm, ffn1_wg, ffn1_wu, ffn1_wd, mix_norm, w_in, gmlp_v_norm, gmlp_ws, gmlp_bs, w_out, ffn2_norm, ffn2_wg, ffn2_wu, ffn2_wd, ple_norm, ple_w_gate, ple_w_proj, final_norm):
    weights = dict(ffn1_norm=ffn1_norm, ffn1_wg=ffn1_wg, ffn1_wu=ffn1_wu, ffn1_wd=ffn1_wd,
                   mix_norm=mix_norm, w_in=w_in, gmlp_v_norm=gmlp_v_norm, gmlp_ws=gmlp_ws,
                   gmlp_bs=gmlp_bs, w_out=w_out, ffn2_norm=ffn2_norm, ffn2_wg=ffn2_wg,
                   ffn2_wu=ffn2_wu, ffn2_wd=ffn2_wd, ple_norm=ple_norm,
                   ple_w_gate=ple_w_gate, ple_w_proj=ple_w_proj)
    depth = w_in.shape[0]
    bp, sp, d = x_prompt.shape
    bs, ss, _ = x_sample.shape
    assert bp == 1, "prompt attention handles one sequence"
    heads = cache_k.shape[3]
    width = heads * SB_HEAD_DIM

    hp = x_prompt.reshape(bp * sp, d)
    hs = x_sample.reshape(bs * ss, d)
    kp_l, vp_l, ks_l, vs_l, zs_l = [], [], [], [], []
    for i in range(depth):
        w = {name: (val[i].astype(BF16) if name in _PRECAST_WEIGHTS else val[i])
             for name, val in weights.items()}
        fgain = final_norm if i == depth - 1 else None
        cache = (cache_k[i].reshape(bs, -1, SB_HEAD_DIM), cache_v[i].reshape(bs, -1, SB_HEAD_DIM))
        hs, k_s, v_s, z_s, w = _layer(hs, p_sample[i].reshape(bs * ss, -1), w, seq_len=ss,
                                      final_gain=fgain, cache=cache)
        hp, kp, vp, _, _ = _layer(hp, p_prompt[i].reshape(bp * sp, -1), w, seq_len=sp,
                                  final_gain=fgain)
        kp_l.append(kp.reshape(bp, sp, heads, SB_HEAD_DIM))
        vp_l.append(vp.reshape(bp, sp, heads, SB_HEAD_DIM))
        ks_l.append(k_s.reshape(bs, ss, heads, SB_HEAD_DIM))
        vs_l.append(v_s.reshape(bs, ss, heads, SB_HEAD_DIM))
        zs_l.append(z_s.reshape(bs, ss, -1, MLP_GROUP_DIM))
    return (hp.reshape(bp, sp, d), hs.reshape(bs, ss, d),
            jnp.stack(kp_l), jnp.stack(vp_l), jnp.stack(ks_l), jnp.stack(vs_l), jnp.stack(zs_l))
```

```python
import functools

import jax
import jax.numpy as jnp
from jax import lax
from jax.experimental import pallas as pl
from jax.experimental.pallas import tpu as pltpu

F32 = jnp.float32
BF16 = jnp.bfloat16

EPS = 1e-6
SB_HEAD_DIM = 128
MLP_GROUP_DIM = 128
MLP_CHUNK = 128

V7X_VMEM_BYTES = 64 * 1024 * 1024
VMEM_REQUEST_CAP = V7X_VMEM_BYTES - 6 * 1024 * 1024
LANES = 128

DEAD_LOG2 = 152.0
LOG2_E = 1.4426950408889634

ROW_TILE = 512
FFN_ROW_TILE = 1024
FFN_COL_TILE = 512
FFN_CAST_COL_TILE = 256
ATT_BLOCK = 256
ATT_HEADS_PER_STEP = 4
ATT_QBLOCKS_PER_STEP = 2


def _compiler_params(semantics, vmem_bytes):
    limit = min(int(vmem_bytes * 1.2) + (4 << 20), VMEM_REQUEST_CAP)
    return pltpu.CompilerParams(dimension_semantics=semantics, vmem_limit_bytes=limit)


def _rms_scale(x):
    return lax.rsqrt(jnp.mean(x * x, axis=-1, keepdims=True) + EPS)


def _bf16_weight(w_ref, cast_ref):
    if cast_ref is None:
        return w_ref[...]
    w = w_ref[...].astype(BF16)
    cast_ref[...] = w
    return w


def _ffn_kernel(x_ref, gain_ref, wg_ref, wu_ref, wd_ref, o_ref, *rest, nf, cast_weights):
    if cast_weights:
        wg_cast, wu_cast, wd_cast, n_ref, a0_ref, a1_ref = rest
    else:
        n_ref, a0_ref, a1_ref = rest
        wg_cast = wu_cast = wd_cast = None
    j = pl.program_id(1)

    def up(a_ref):
        n = n_ref[...]
        g = jnp.dot(n, _bf16_weight(wg_ref, wg_cast), preferred_element_type=F32)
        u = jnp.dot(n, _bf16_weight(wu_ref, wu_cast), preferred_element_type=F32)
        a_ref[...] = (0.5 * g * jax.nn.sigmoid(g) * u).astype(BF16)

    def down(a_ref):
        o_ref[...] += jnp.dot(a_ref[...], _bf16_weight(wd_ref, wd_cast),
                              preferred_element_type=F32)

    @pl.when(j == 0)
    def _():
        x = x_ref[...]
        n_ref[...] = (x * _rms_scale(x) * gain_ref[...]).astype(BF16)
        o_ref[...] = x
        up(a0_ref)

    middle = (j > 0) & (j < nf)

    @pl.when(middle & (j % 2 == 1))
    def _():
        down(a0_ref)
        up(a1_ref)

    @pl.when(middle & (j % 2 == 0))
    def _():
        down(a1_ref)
        up(a0_ref)

    @pl.when(j == nf)
    def _():
        down(a1_ref if (nf - 1) % 2 else a0_ref)


def _ffn(x, gain, wg, wu, wd):
    t, d = x.shape
    f = wg.shape[1]
    cast_weights = wg.dtype != BF16
    tm = min(FFN_ROW_TILE, t)
    tf = FFN_CAST_COL_TILE if cast_weights else FFN_COL_TILE
    assert t % tm == 0 and f % tf == 0
    assert not cast_weights or t == tm
    nf = f // tf
    wbytes = jnp.dtype(wg.dtype).itemsize
    x_bufs = 1 if cast_weights else 2
    vmem = ((x_bufs + 2) * tm * d * 4 + tm * d * 2 + 2 * 3 * d * tf * wbytes
            + 2 * tm * tf * 2 + 2 * tm * tf * 4)
    up_blk = lambda i, j: (0, jnp.minimum(j, nf - 1))
    down_blk = lambda i, j: (jnp.maximum(j - 1, 0), 0)
    row_blk = lambda i, j: (i, 0)
    out_specs = [pl.BlockSpec((tm, d), row_blk)]
    out_shape = [jax.ShapeDtypeStruct((t, d), F32)]
    if cast_weights:
        vmem += 2 * 3 * d * tf * 2
        out_specs += [pl.BlockSpec((d, tf), up_blk), pl.BlockSpec((d, tf), up_blk),
                      pl.BlockSpec((tf, d), down_blk)]
        out_shape += [jax.ShapeDtypeStruct(w.shape, BF16) for w in (wg, wu, wd)]
    x_spec = (pl.BlockSpec((tm, d), row_blk, pipeline_mode=pl.Buffered(1)) if cast_weights
              else pl.BlockSpec((tm, d), row_blk))
    outs = pl.pallas_call(
        functools.partial(_ffn_kernel, nf=nf, cast_weights=cast_weights),
        grid=(t // tm, nf + 1),
        in_specs=[
            x_spec,
            pl.BlockSpec((1, d), lambda i, j: (0, 0)),
            pl.BlockSpec((d, tf), up_blk),
            pl.BlockSpec((d, tf), up_blk),
            pl.BlockSpec((tf, d), down_blk),
        ],
        out_specs=out_specs,
        out_shape=out_shape,
        scratch_shapes=[pltpu.VMEM((tm, d), BF16), pltpu.VMEM((tm, tf), BF16),
                        pltpu.VMEM((tm, tf), BF16)],
        compiler_params=_compiler_params(("parallel", "arbitrary"), vmem),
        name="ffn",
    )(x, gain.reshape(1, d), wg, wu, wd)
    return (outs[0], tuple(outs[1:])) if cast_weights else (outs[0], (wg, wu, wd))


INPROJ_GROUPS = dict(q=0, k=1, v=2, u=3, z=4)


def _gmlp_mix(u, zn, ws_ref, bst_ref, om_ref):
    znb = zn.astype(BF16)
    row = lax.broadcasted_iota(jnp.int32, (MLP_CHUNK, MLP_CHUNK), 0)
    col = lax.broadcasted_iota(jnp.int32, (MLP_CHUNK, MLP_CHUNK), 1)
    lower = col <= row
    for g in range(ws_ref.shape[0]):
        w = jnp.where(lower, ws_ref[g], 0.0).astype(BF16)
        bias = bst_ref[:, g:g + 1]
        cs = slice(g * MLP_GROUP_DIM, (g + 1) * MLP_GROUP_DIM)
        for c in range(znb.shape[0] // MLP_CHUNK):
            rs = slice(c * MLP_CHUNK, (c + 1) * MLP_CHUNK)
            mixed = jnp.dot(w, znb[rs, cs], preferred_element_type=F32) + bias
            om_ref[rs, cs] = (u[rs, cs] * mixed).astype(BF16)


def _inproj_kernel(h_ref, gain_ref, w_ref, vgain_ref, ws_ref, bst_ref,
                   q_ref, k_ref, v_ref, kb_ref, vb_ref, om_ref, *zn_out):
    width = q_ref.shape[1]
    h = h_ref[...]
    n = (h * _rms_scale(h) * gain_ref[...]).astype(BF16)

    def project(name):
        g = INPROJ_GROUPS[name]
        return jnp.dot(n, w_ref[:, g * width:(g + 1) * width], preferred_element_type=F32)

    u = jax.nn.gelu(project("u"), approximate=True)
    gz = jax.nn.gelu(project("z"), approximate=True)
    zn = gz * _rms_scale(gz) * vgain_ref[...]
    if zn_out:
        zn_out[0][...] = zn
    _gmlp_mix(u, zn, ws_ref, bst_ref, om_ref)
    q_ref[...] = (project("q") * (SB_HEAD_DIM ** -0.5 * LOG2_E)).astype(BF16)
    k = project("k")
    k_ref[...] = k
    kb_ref[...] = k.astype(BF16)
    v = project("v")
    v_ref[...] = v
    vb_ref[...] = v.astype(BF16)


def _inproj(h, gain, w_in, vgain, ws, bst, *, emit_zn):
    t, d = h.shape
    width = w_in.shape[1] // len(INPROJ_GROUPS)
    tm = min(ROW_TILE, t)
    assert t % tm == 0 and tm % MLP_CHUNK == 0 and w_in.shape[1] == len(INPROJ_GROUPS) * width
    groups = ws.shape[0]
    row_blk = lambda i: (i, 0)
    const2 = lambda i: (0, 0)
    out_shape = [
        jax.ShapeDtypeStruct((t, width), BF16),
        jax.ShapeDtypeStruct((t, width), F32),
        jax.ShapeDtypeStruct((t, width), F32),
        jax.ShapeDtypeStruct((t, width), BF16),
        jax.ShapeDtypeStruct((t, width), BF16),
        jax.ShapeDtypeStruct((t, width), BF16),
    ]
    if emit_zn:
        out_shape.append(jax.ShapeDtypeStruct((t, width), F32))
    out_specs = [pl.BlockSpec((tm, width), row_blk) for _ in out_shape]
    out_bytes = sum(tm * width * jnp.dtype(s.dtype).itemsize for s in out_shape)
    vmem = (2 * tm * d * 4 + w_in.size * 2 + 2 * out_bytes + tm * d * 2 + 6 * tm * width * 4)
    return pl.pallas_call(
        _inproj_kernel,
        grid=(t // tm,),
        in_specs=[
            pl.BlockSpec((tm, d), row_blk),
            pl.BlockSpec((1, d), const2),
            pl.BlockSpec(w_in.shape, const2, pipeline_mode=pl.Buffered(1)),
            pl.BlockSpec((1, width), const2),
            pl.BlockSpec((groups, MLP_CHUNK, MLP_CHUNK), lambda i: (0, 0, 0)),
            pl.BlockSpec((MLP_CHUNK, groups), const2),
        ],
        out_specs=out_specs,
        out_shape=out_shape,
        compiler_params=_compiler_params(("parallel",), vmem),
        name="inproj",
    )(h, gain.reshape(1, d), w_in, vgain.reshape(1, width), ws, bst)


def _init_tri(tri_ref):
    j = lax.broadcasted_iota(jnp.int32, tri_ref.shape, 0)
    s = lax.broadcasted_iota(jnp.int32, tri_ref.shape, 1)
    tri_ref[...] = jnp.where(j > s, -1.0, 0.0).astype(BF16)


def _sb_span(q, k, v, valids, tri, carry):
    tb = tri.shape[1]
    u = lax.dot_general(q, k, (((1,), (1,)), ((), ())), preferred_element_type=F32)
    s = jnp.maximum(u, 0.0) + jnp.log2(1.0 + jnp.exp2(jnp.minimum(u, -u)))
    weights = [None] * len(valids)
    for b in reversed(range(len(valids))):
        cols = slice(b * tb, (b + 1) * tb)
        sb = s[:, cols]
        if valids[b] is not None:
            sb = jnp.where(valids[b], sb, 0.0)
        newer = jnp.dot(sb.astype(BF16), tri, preferred_element_type=F32)
        log2_a = u[:, cols] - sb + newer
        total = jnp.sum(sb, axis=1, keepdims=True)
        if carry is None:
            carry = total
        else:
            log2_a = log2_a - carry
            carry = carry + total
        a = jnp.exp2(log2_a)
        if valids[b] is not None:
            a = jnp.where(valids[b], a, 0.0)
        weights[b] = a.astype(BF16)
    a = weights[0] if len(weights) == 1 else jnp.concatenate(weights, axis=1)
    return jnp.dot(a, v, preferred_element_type=F32), carry


def _head_rows(ref, h, rows):
    return ref.at[pl.ds(h * rows, rows)]


def _sb_older_blocks(q, load_kv, first_kb, tri, carry_ref, acc_ref):
    def cond(state):
        kb, low = state
        return (kb >= 0) & (low < DEAD_LOG2)

    def body(state):
        kb, _ = state
        k, v = load_kv(kb)
        o, carry = _sb_span(q, k, v, [None], tri, carry_ref[...])
        acc_ref[...] += o
        carry_ref[...] = carry
        return kb - 1, jnp.min(carry)

    return lax.while_loop(cond, body, (first_kb, jnp.min(carry_ref[...])))[1]


def _attn_prompt_kernel(q_ref, k_ref, v_ref, o_ref, tri_ref, carry_ref, acc_ref):
    i = pl.program_id(1)
    tb = tri_ref.shape[0]
    nqb = q_ref.shape[0] // tb
    heads = q_ref.shape[1] // SB_HEAD_DIM

    @pl.when((pl.program_id(0) == 0) & (i == 0))
    def _():
        _init_tri(tri_ref)

    tri = tri_ref[...]
    causal = (lax.broadcasted_iota(jnp.int32, (tb, tb), 1)
              < lax.broadcasted_iota(jnp.int32, (tb, tb), 0))
    chains = [(b, h) for b in range(nqb) for h in range(heads)]

    def q_rows(b):
        return slice(b * tb, (b + 1) * tb)

    def head_cols(h):
        return slice(h * SB_HEAD_DIM, (h + 1) * SB_HEAD_DIM)

    def first_span(b, h, with_previous_block):
        hc = head_cols(h)
        qblock = i * nqb + b
        if with_previous_block:
            keys = pl.ds(pl.multiple_of((qblock - 1) * tb, tb), 2 * tb)
            valids = [None, causal]
        else:
            keys = pl.ds(pl.multiple_of(qblock * tb, tb), tb)
            valids = [causal]
        o, carry = _sb_span(q_ref[q_rows(b), hc], k_ref[keys, hc], v_ref[keys, hc], valids,
                            tri, None)
        slot = b * heads + h
        o_ref[q_rows(b), hc] = o.astype(o_ref.dtype)
        _head_rows(acc_ref, slot, tb)[...] = o
        _head_rows(carry_ref, slot, tb)[...] = carry
        return carry

    def run(first_block_has_previous):
        lowest = None
        for b, h in chains:
            carry = first_span(b, h, first_block_has_previous or b > 0)
            lowest = carry if lowest is None else jnp.minimum(lowest, carry)

        @pl.when(jnp.min(lowest) < DEAD_LOG2)
        def _():
            for b, h in chains:
                hc = head_cols(h)
                slot = b * heads + h

                def load_kv(kb, hc=hc):
                    rows = pl.ds(pl.multiple_of(kb * tb, tb), tb)
                    return k_ref[rows, hc], v_ref[rows, hc]

                acc = _head_rows(acc_ref, slot, tb)
                _sb_older_blocks(q_ref[q_rows(b), hc], load_kv, i * nqb + b - 2, tri,
                                 _head_rows(carry_ref, slot, tb), acc)
                o_ref[q_rows(b), hc] = acc[...].astype(o_ref.dtype)

    @pl.when(i == 0)
    def _():
        run(False)

    @pl.when(i > 0)
    def _():
        run(True)


def _attn_prompt(q, k, v):
    s, width = q.shape
    hb = ATT_HEADS_PER_STEP
    tb = ATT_BLOCK
    tq = ATT_QBLOCKS_PER_STEP * tb
    assert s % tq == 0 and width % (hb * SB_HEAD_DIM) == 0
    wb = hb * SB_HEAD_DIM
    chains = ATT_QBLOCKS_PER_STEP * hb
    resident = pl.Buffered(1)
    vmem = 2 * s * wb * 2 + 2 * 2 * tq * wb * 2 + tb * tb * 2 + chains * 8 * tb * 2 * tb * 4
    return pl.pallas_call(
        _attn_prompt_kernel,
        grid=(width // wb, s // tq),
        in_specs=[
            pl.BlockSpec((tq, wb), lambda g, i: (i, g)),
            pl.BlockSpec((s, wb), lambda g, i: (0, g), pipeline_mode=resident),
            pl.BlockSpec((s, wb), lambda g, i: (0, g), pipeline_mode=resident),
        ],
        out_specs=pl.BlockSpec((tq, wb), lambda g, i: (i, g)),
        out_shape=jax.ShapeDtypeStruct((s, width), BF16),
        scratch_shapes=[
            pltpu.VMEM((tb, tb), BF16),
            pltpu.VMEM((chains * tb, 1), F32),
            pltpu.VMEM((chains * tb, SB_HEAD_DIM), F32),
        ],
        compiler_params=_compiler_params(("arbitrary", "arbitrary"), vmem),
        name="attn_prompt",
    )(q, k, v)


def _attn_sample_kernel(q_ref, kn_ref, vn_ref, ck_ref, cv_ref, o_ref,
                        tri_ref, carry_ref, acc_ref, *, heads):
    tq = q_ref.shape[0]
    past = ck_ref.shape[1] // heads
    tb = tri_ref.shape[1]
    head_rows = tb - tq
    nfull = (past - head_rows) // tb
    rem = past - head_rows - nfull * tb

    @pl.when(pl.program_id(0) == 0)
    def _():
        _init_tri(tri_ref)

    tri = tri_ref[...]
    row = lax.broadcasted_iota(jnp.int32, (tq, tb), 0)
    col = lax.broadcasted_iota(jnp.int32, (tq, tb), 1)

    def cache_rows(ref, h, start, size):
        return ref[0, pl.ds(start * heads + h, size, stride=heads), :].astype(BF16)

    lowest = None
    for h in range(heads):
        hc = slice(h * SB_HEAD_DIM, (h + 1) * SB_HEAD_DIM)
        k0 = jnp.concatenate([cache_rows(ck_ref, h, past - head_rows, head_rows), kn_ref[:, hc]],
                             axis=0)
        v0 = jnp.concatenate([cache_rows(cv_ref, h, past - head_rows, head_rows), vn_ref[:, hc]],
                             axis=0)
        o, carry = _sb_span(q_ref[:, hc], k0, v0, [col < row + head_rows], tri, None)
        o_ref[:, hc] = o.astype(o_ref.dtype)
        _head_rows(acc_ref, h, tq)[...] = o
        _head_rows(carry_ref, h, tq)[...] = carry
        lowest = carry if lowest is None else jnp.minimum(lowest, carry)

    @pl.when(jnp.min(lowest) < DEAD_LOG2)
    def _():
        for h in range(heads):
            hc = slice(h * SB_HEAD_DIM, (h + 1) * SB_HEAD_DIM)
            q = q_ref[:, hc]
            acc = _head_rows(acc_ref, h, tq)
            carry_h = _head_rows(carry_ref, h, tq)

            def load_kv(kb, h=h):
                start = rem + kb * tb
                return cache_rows(ck_ref, h, start, tb), cache_rows(cv_ref, h, start, tb)

            low = _sb_older_blocks(q, load_kv, nfull - 1, tri, carry_h, acc)

            if rem:
                @pl.when(low < DEAD_LOG2)
                def _():
                    o, _ = _sb_span(q, cache_rows(ck_ref, h, 0, tb), cache_rows(cv_ref, h, 0, tb),
                                    [col < rem], tri, carry_h[...])
                    acc[...] += o

            o_ref[:, hc] = acc[...].astype(o_ref.dtype)


def _attn_sample(q, kn, vn, cache_k, cache_v, batch, heads):
    rows, width = q.shape
    tq = rows // batch
    past = cache_k.shape[1] // heads
    tb = ATT_BLOCK
    assert tq < tb and past >= 2 * tb and tq % 16 == 0 and width == heads * SB_HEAD_DIM
    new_blk = lambda b: (b, 0)
    cache_blk = lambda b: (b, 0, 0)
    vmem = 2 * 2 * past * width * 4 + 2 * tb * tb * 2 + heads * 14 * tq * tb * 4
    return pl.pallas_call(
        functools.partial(_attn_sample_kernel, heads=heads),
        grid=(batch,),
        in_specs=[
            pl.BlockSpec((tq, width), new_blk),
            pl.BlockSpec((tq, width), new_blk),
            pl.BlockSpec((tq, width), new_blk),
            pl.BlockSpec((1, past * heads, SB_HEAD_DIM), cache_blk),
            pl.BlockSpec((1, past * heads, SB_HEAD_DIM), cache_blk),
        ],
        out_specs=pl.BlockSpec((tq, width), new_blk),
        out_shape=jax.ShapeDtypeStruct((rows, width), BF16),
        scratch_shapes=[
            pltpu.VMEM((tb, tb), BF16),
            pltpu.VMEM((heads * tq, 1), F32),
            pltpu.VMEM((heads * tq, SB_HEAD_DIM), F32),
        ],
        compiler_params=_compiler_params(("arbitrary",), vmem),
        name="attn_sample",
    )(q, kn, vn, cache_k, cache_v)


def _outproj_kernel(h_ref, osb_ref, om_ref, wa_ref, wb_ref, o_ref):
    o_ref[...] = (h_ref[...]
                  + jnp.dot(osb_ref[...], wa_ref[...], preferred_element_type=F32)
                  + jnp.dot(om_ref[...], wb_ref[...], preferred_element_type=F32))


def _outproj(h, osb, om, w_out):
    t, d = h.shape
    width = osb.shape[1]
    tm = min(ROW_TILE, t)
    assert t % tm == 0 and w_out.shape == (2 * width, d)
    vmem = 2 * 2 * tm * d * 4 + 2 * 2 * tm * width * 2 + 2 * 2 * width * d * 2
    return pl.pallas_call(
        _outproj_kernel,
        grid=(t // tm,),
        in_specs=[
            pl.BlockSpec((tm, d), lambda i: (i, 0)),
            pl.BlockSpec((tm, width), lambda i: (i, 0)),
            pl.BlockSpec((tm, width), lambda i: (i, 0)),
            pl.BlockSpec((width, d), lambda i: (0, 0)),
            pl.BlockSpec((width, d), lambda i: (1, 0)),
        ],
        out_specs=pl.BlockSpec((tm, d), lambda i: (i, 0)),
        out_shape=jax.ShapeDtypeStruct((t, d), F32),
        compiler_params=_compiler_params(("parallel",), vmem),
        name="outproj",
    )(h, osb, om, w_out, w_out)


def _ple_kernel(h_ref, p_ref, gain_ref, wg_ref, wp_ref, fgain_ref, o_ref, *, final_norm):
    h = h_ref[...]
    n = (h * _rms_scale(h) * gain_ref[...]).astype(BF16)
    gate = jax.nn.sigmoid(jnp.dot(n, wg_ref[...], preferred_element_type=F32))
    proj = jnp.dot(p_ref[...].astype(BF16), wp_ref[...], preferred_element_type=F32)
    o = h + gate * proj
    if final_norm:
        o = o * _rms_scale(o) * fgain_ref[...]
    o_ref[...] = o


def _ple(h, p, gain, w_gate, w_proj, fgain, *, final_norm):
    t, d = h.shape
    pd = p.shape[1]
    tm = min(ROW_TILE, t)
    assert t % tm == 0
    vmem = 2 * 2 * tm * d * 4 + 2 * tm * pd * 4 + 2 * d * d * 2 + 2 * pd * d * 2 + 3 * tm * d * 4
    return pl.pallas_call(
        functools.partial(_ple_kernel, final_norm=final_norm),
        grid=(t // tm,),
        in_specs=[
            pl.BlockSpec((tm, d), lambda i: (i, 0)),
            pl.BlockSpec((tm, pd), lambda i: (i, 0)),
            pl.BlockSpec((1, d), lambda i: (0, 0)),
            pl.BlockSpec((d, d), lambda i: (0, 0)),
            pl.BlockSpec((pd, d), lambda i: (0, 0)),
            pl.BlockSpec((1, d), lambda i: (0, 0)),
        ],
        out_specs=pl.BlockSpec((tm, d), lambda i: (i, 0)),
        out_shape=jax.ShapeDtypeStruct((t, d), F32),
        compiler_params=_compiler_params(("parallel",), vmem),
        name="ple",
    )(h, p, gain.reshape(1, d), w_gate, w_proj, fgain.reshape(1, d))


def _gmlp_weights(ws, bs, seq_len):
    length = min(seq_len, MLP_CHUNK)
    assert MLP_CHUNK % length == 0
    reps = MLP_CHUNK // length
    if reps > 1:
        eye = jnp.eye(reps, dtype=ws.dtype)
        ws = jnp.einsum("ab,gij->gaibj", eye, ws[:, :length, :length]).reshape(
            ws.shape[0], MLP_CHUNK, MLP_CHUNK)
        bs = jnp.tile(bs[:, :length], (1, reps))
    return ws, jnp.transpose(bs)


def _layer(x, p, w, *, seq_len, final_gain, cache=None):
    rows = x.shape[0]
    w = dict(w)
    h, (w["ffn1_wg"], w["ffn1_wu"], w["ffn1_wd"]) = _ffn(
        x, w["ffn1_norm"], w["ffn1_wg"], w["ffn1_wu"], w["ffn1_wd"])
    ws, bst = _gmlp_weights(w["gmlp_ws"], w["gmlp_bs"], seq_len)
    outs = _inproj(h, w["mix_norm"], w["w_in"], w["gmlp_v_norm"], ws, bst,
                   emit_zn=cache is not None)
    q, k, v, kb, vb, om = outs[:6]
    if cache is None:
        osb = _attn_prompt(q, kb, vb)
        zn = None
    else:
        osb = _attn_sample(q, kb, vb, cache[0], cache[1], rows // seq_len,
                           q.shape[1] // SB_HEAD_DIM)
        zn = outs[6]
    h = _outproj(h, osb, om, w["w_out"])
    h, (w["ffn2_wg"], w["ffn2_wu"], w["ffn2_wd"]) = _ffn(
        h, w["ffn2_norm"], w["ffn2_wg"], w["ffn2_wu"], w["ffn2_wd"])
    h = _ple(h, p, w["ple_norm"], w["ple_w_gate"], w["ple_w_proj"],
             final_gain if final_gain is not None else w["ple_norm"],
             final_norm=final_gain is not None)
    return h, k, v, zn, w


_PRECAST_WEIGHTS = ("w_in", "w_out", "ple_w_gate", "ple_w_proj")


def kernel(x_prompt, x_sample, cache_k, cache_v, p_prompt, p_sample, ffn1_norm, ffn1_wg, ffn1_wu, ffn1_wd, mix_norm, w_in, gmlp_v_norm, gmlp_ws, gmlp_bs, w_out, ffn2_norm, ffn2_wg, ffn2_wu, ffn2_wd, ple_norm, ple_w_gate, ple_w_proj, final_norm):
    weights = dict(ffn1_norm=ffn1_norm, ffn1_wg=ffn1_wg, ffn1_wu=ffn1_wu, ffn1_wd=ffn1_wd,
                   mix_norm=mix_norm, w_in=w_in, gmlp_v_norm=gmlp_v_norm, gmlp_ws=gmlp_ws,
                   gmlp_bs=gmlp_bs, w_out=w_out, ffn2_norm=ffn2_norm, ffn2_wg=ffn2_wg,
                   ffn2_wu=ffn2_wu, ffn2_wd=ffn2_wd, ple_norm=ple_norm,
                   ple_w_gate=ple_w_gate, ple_w_proj=ple_w_proj)
    depth = w_in.shape[0]
    bp, sp, d = x_prompt.shape
    bs, ss, _ = x_sample.shape
    assert bp == 1, "prompt attention handles one sequence"
    heads = cache_k.shape[3]
    width = heads * SB_HEAD_DIM

    hp = x_prompt.reshape(bp * sp, d)
    hs = x_sample.reshape(bs * ss, d)
    kp_l, vp_l, ks_l, vs_l, zs_l = [], [], [], [], []
    for i in range(depth):
        w = {name: (val[i].astype(BF16) if name in _PRECAST_WEIGHTS else val[i])
             for name, val in weights.items()}
        fgain = final_norm if i == depth - 1 else None
        cache = (cache_k[i].reshape(bs, -1, SB_HEAD_DIM), cache_v[i].reshape(bs, -1, SB_HEAD_DIM))
        hs, k_s, v_s, z_s, w = _layer(hs, p_sample[i].reshape(bs * ss, -1), w, seq_len=ss,
                                      final_gain=fgain, cache=cache)
        hp, kp, vp, _, _ = _layer(hp, p_prompt[i].reshape(bp * sp, -1), w, seq_len=sp,
                                  final_gain=fgain)
        kp_l.append(kp.reshape(bp, sp, heads, SB_HEAD_DIM))
        vp_l.append(vp.reshape(bp, sp, heads, SB_HEAD_DIM))
        ks_l.append(k_s.reshape(bs, ss, heads, SB_HEAD_DIM))
        vs_l.append(v_s.reshape(bs, ss, heads, SB_HEAD_DIM))
        zs_l.append(z_s.reshape(bs, ss, -1, MLP_GROUP_DIM))
    return (hp.reshape(bp, sp, d), hs.reshape(bs, ss, d),
            jnp.stack(kp_l), jnp.stack(vp_l), jnp.stack(ks_l), jnp.stack(vs_l), jnp.stack(zs_l))
```

```python
import functools

import jax
import jax.numpy as jnp
from jax import lax
from jax.experimental import pallas as pl
from jax.experimental.pallas import tpu as pltpu

F32 = jnp.float32
BF16 = jnp.bfloat16

EPS = 1e-6
SB_HEAD_DIM = 128
MLP_GROUP_DIM = 128
MLP_CHUNK = 128

V7X_VMEM_BYTES = 64 * 1024 * 1024
VMEM_REQUEST_CAP = V7X_VMEM_BYTES - 6 * 1024 * 1024
LANES = 128

DEAD_LOG2 = 152.0
LOG2_E = 1.4426950408889634

ROW_TILE = 512
FFN_ROW_TILE = 1024
FFN_COL_TILE = 512
FFN_CAST_COL_TILE = 256
ATT_BLOCK = 256
ATT_HEADS_PER_STEP = 4
ATT_QBLOCKS_PER_STEP = 2


def _compiler_params(semantics, vmem_bytes):
    limit = min(int(vmem_bytes * 1.2) + (4 << 20), VMEM_REQUEST_CAP)
    return pltpu.CompilerParams(dimension_semantics=semantics, vmem_limit_bytes=limit)


def _rms_scale(x):
    return lax.rsqrt(jnp.mean(x * x, axis=-1, keepdims=True) + EPS)


def _bf16_weight(w_ref, cast_ref):
    if cast_ref is None:
        return w_ref[...]
    w = w_ref[...].astype(BF16)
    cast_ref[...] = w
    return w


def _ffn_kernel(x_ref, gain_ref, wg_ref, wu_ref, wd_ref, o_ref, *rest, nf, cast_weights):
    if cast_weights:
        wg_cast, wu_cast, wd_cast, n_ref, a0_ref, a1_ref = rest
    else:
        n_ref, a0_ref, a1_ref = rest
        wg_cast = wu_cast = wd_cast = None
    j = pl.program_id(1)

    def up(a_ref):
        n = n_ref[...]
        g = jnp.dot(n, _bf16_weight(wg_ref, wg_cast), preferred_element_type=F32)
        u = jnp.dot(n, _bf16_weight(wu_ref, wu_cast), preferred_element_type=F32)
        a_ref[...] = (0.5 * g * jax.nn.sigmoid(g) * u).astype(BF16)

    def down(a_ref):
        o_ref[...] += jnp.dot(a_ref[...], _bf16_weight(wd_ref, wd_cast),
                              preferred_element_type=F32)

    @pl.when(j == 0)
    def _():
        x = x_ref[...]
        n_ref[...] = (x * _rms_scale(x) * gain_ref[...]).astype(BF16)
        o_ref[...] = x
        up(a0_ref)

    middle = (j > 0) & (j < nf)

    @pl.when(middle & (j % 2 == 1))
    def _():
        down(a0_ref)
        up(a1_ref)

    @pl.when(middle & (j % 2 == 0))
    def _():
        down(a1_ref)
        up(a0_ref)

    @pl.when(j == nf)
    def _():
        down(a1_ref if (nf - 1) % 2 else a0_ref)


def _ffn(x, gain, wg, wu, wd):
    t, d = x.shape
    f = wg.shape[1]
    cast_weights = wg.dtype != BF16
    tm = min(FFN_ROW_TILE, t)
    tf = FFN_CAST_COL_TILE if cast_weights else FFN_COL_TILE
    assert t % tm == 0 and f % tf == 0
    assert not cast_weights or t == tm
    nf = f // tf
    wbytes = jnp.dtype(wg.dtype).itemsize
    x_bufs = 1 if cast_weights else 2
    vmem = ((x_bufs + 2) * tm * d * 4 + tm * d * 2 + 2 * 3 * d * tf * wbytes
            + 2 * tm * tf * 2 + 2 * tm * tf * 4)
    up_blk = lambda i, j: (0, jnp.minimum(j, nf - 1))
    down_blk = lambda i, j: (jnp.maximum(j - 1, 0), 0)
    row_blk = lambda i, j: (i, 0)
    out_specs = [pl.BlockSpec((tm, d), row_blk)]
    out_shape = [jax.ShapeDtypeStruct((t, d), F32)]
    if cast_weights:
        vmem += 2 * 3 * d * tf * 2
        out_specs += [pl.BlockSpec((d, tf), up_blk), pl.BlockSpec((d, tf), up_blk),
                      pl.BlockSpec((tf, d), down_blk)]
        out_shape += [jax.ShapeDtypeStruct(w.shape, BF16) for w in (wg, wu, wd)]
    x_spec = (pl.BlockSpec((tm, d), row_blk, pipeline_mode=pl.Buffered(1)) if cast_weights
              else pl.BlockSpec((tm, d), row_blk))
    outs = pl.pallas_call(
        functools.partial(_ffn_kernel, nf=nf, cast_weights=cast_weights),
        grid=(t // tm, nf + 1),
        in_specs=[
            x_spec,
            pl.BlockSpec((1, d), lambda i, j: (0, 0)),
            pl.BlockSpec((d, tf), up_blk),
            pl.BlockSpec((d, tf), up_blk),
            pl.BlockSpec((tf, d), down_blk),
        ],
        out_specs=out_specs,
        out_shape=out_shape,
        scratch_shapes=[pltpu.VMEM((tm, d), BF16), pltpu.VMEM((tm, tf), BF16),
                        pltpu.VMEM((tm, tf), BF16)],
        compiler_params=_compiler_params(("parallel", "arbitrary"), vmem),
        name="ffn",
    )(x, gain.reshape(1, d), wg, wu, wd)
    return (outs[0], tuple(outs[1:])) if cast_weights else (outs[0], (wg, wu, wd))


INPROJ_GROUPS = dict(q=0, k=1, v=2, u=3, z=4)


def _gmlp_mix(u, zn, ws_ref, bst_ref, om_ref):
    znb = zn.astype(BF16)
    row = lax.broadcasted_iota(jnp.int32, (MLP_CHUNK, MLP_CHUNK), 0)
    col = lax.broadcasted_iota(jnp.int32, (MLP_CHUNK, MLP_CHUNK), 1)
    lower = col <= row
    for g in range(ws_ref.shape[0]):
        w = jnp.where(lower, ws_ref[g], 0.0).astype(BF16)
        bias = bst_ref[:, g:g + 1]
        cs = slice(g * MLP_GROUP_DIM, (g + 1) * MLP_GROUP_DIM)
        for c in range(znb.shape[0] // MLP_CHUNK):
            rs = slice(c * MLP_CHUNK, (c + 1) * MLP_CHUNK)
            mixed = jnp.dot(w, znb[rs, cs], preferred_element_type=F32) + bias
            om_ref[rs, cs] = (u[rs, cs] * mixed).astype(BF16)


def _inproj_kernel(h_ref, gain_ref, w_ref, vgain_ref, ws_ref, bst_ref,
                   q_ref, k_ref, v_ref, kb_ref, vb_ref, om_ref, *zn_out):
    width = q_ref.shape[1]
    h = h_ref[...]
    n = (h * _rms_scale(h) * gain_ref[...]).astype(BF16)

    def project(name):
        g = INPROJ_GROUPS[name]
        return jnp.dot(n, w_ref[:, g * width:(g + 1) * width], preferred_element_type=F32)

    u = jax.nn.gelu(project("u"), approximate=True)
    gz = jax.nn.gelu(project("z"), approximate=True)
    zn = gz * _rms_scale(gz) * vgain_ref[...]
    if zn_out:
        zn_out[0][...] = zn
    _gmlp_mix(u, zn, ws_ref, bst_ref, om_ref)
    q_ref[...] = (project("q") * (SB_HEAD_DIM ** -0.5 * LOG2_E)).astype(BF16)
    k = project("k")
    k_ref[...] = k
    kb_ref[...] = k.astype(BF16)
    v = project("v")
    v_ref[...] = v
    vb_ref[...] = v.astype(BF16)


def _inproj(h, gain, w_in, vgain, ws, bst, *, emit_zn):
    t, d = h.shape
    width = w_in.shape[1] // len(INPROJ_GROUPS)
    tm = min(ROW_TILE, t)
    assert t % tm == 0 and tm % MLP_CHUNK == 0 and w_in.shape[1] == len(INPROJ_GROUPS) * width
    groups = ws.shape[0]
    row_blk = lambda i: (i, 0)
    const2 = lambda i: (0, 0)
    out_shape = [
        jax.ShapeDtypeStruct((t, width), BF16),
        jax.ShapeDtypeStruct((t, width), F32),
        jax.ShapeDtypeStruct((t, width), F32),
        jax.ShapeDtypeStruct((t, width), BF16),
        jax.ShapeDtypeStruct((t, width), BF16),
        jax.ShapeDtypeStruct((t, width), BF16),
    ]
    if emit_zn:
        out_shape.append(jax.ShapeDtypeStruct((t, width), F32))
    out_specs = [pl.BlockSpec((tm, width), row_blk) for _ in out_shape]
    out_bytes = sum(tm * width * jnp.dtype(s.dtype).itemsize for s in out_shape)
    vmem = (2 * tm * d * 4 + w_in.size * 2 + 2 * out_bytes + tm * d * 2 + 6 * tm * width * 4)
    return pl.pallas_call(
        _inproj_kernel,
        grid=(t // tm,),
        in_specs=[
            pl.BlockSpec((tm, d), row_blk),
            pl.BlockSpec((1, d), const2),
            pl.BlockSpec(w_in.shape, const2, pipeline_mode=pl.Buffered(1)),
            pl.BlockSpec((1, width), const2),
            pl.BlockSpec((groups, MLP_CHUNK, MLP_CHUNK), lambda i: (0, 0, 0)),
            pl.BlockSpec((MLP_CHUNK, groups), const2),
        ],
        out_specs=out_specs,
        out_shape=out_shape,
        compiler_params=_compiler_params(("parallel",), vmem),
        name="inproj",
    )(h, gain.reshape(1, d), w_in, vgain.reshape(1, width), ws, bst)


def _init_tri(tri_ref):
    j = lax.broadcasted_iota(jnp.int32, tri_ref.shape, 0)
    s = lax.broadcasted_iota(jnp.int32, tri_ref.shape, 1)
    tri_ref[...] = jnp.where(j > s, -1.0, 0.0).astype(BF16)


def _sb_span(q, k, v, valids, tri, carry):
    tb = tri.shape[1]
    u = lax.dot_general(q, k, (((1,), (1,)), ((), ())), preferred_element_type=F32)
    s = jnp.maximum(u, 0.0) + jnp.log2(1.0 + jnp.exp2(jnp.minimum(u, -u)))
    weights = [None] * len(valids)
    for b in reversed(range(len(valids))):
        cols = slice(b * tb, (b + 1) * tb)
        sb = s[:, cols]
        if valids[b] is not None:
            sb = jnp.where(valids[b], sb, 0.0)
        newer = jnp.dot(sb.astype(BF16), tri, preferred_element_type=F32)
        log2_a = u[:, cols] - sb + newer
        total = jnp.sum(sb, axis=1, keepdims=True)
        if carry is None:
            carry = total
        else:
            log2_a = log2_a - carry
            carry = carry + total
        a = jnp.exp2(log2_a)
        if valids[b] is not None:
            a = jnp.where(valids[b], a, 0.0)
        weights[b] = a.astype(BF16)
    a = weights[0] if len(weights) == 1 else jnp.concatenate(weights, axis=1)
    return jnp.dot(a, v, preferred_element_type=F32), carry


def _head_rows(ref, h, rows):
    return ref.at[pl.ds(h * rows, rows)]


def _sb_older_blocks(q, load_kv, first_kb, tri, carry_ref, acc_ref):
    def cond(state):
        kb, low = state
        return (kb >= 0) & (low < DEAD_LOG2)

    def body(state):
        kb, _ = state
        k, v = load_kv(kb)
        o, carry = _sb_span(q, k, v, [None], tri, carry_ref[...])
        acc_ref[...] += o
        carry_ref[...] = carry
        return kb - 1, jnp.min(carry)

    return lax.while_loop(cond, body, (first_kb, jnp.min(carry_ref[...])))[1]


def _attn_prompt_kernel(q_ref, k_ref, v_ref, o_ref, tri_ref, carry_ref, acc_ref, *, older_only):
    i = pl.program_id(1)
    tb = tri_ref.shape[0]
    nqb = q_ref.shape[0] // tb
    heads = q_ref.shape[1] // SB_HEAD_DIM

    @pl.when((pl.program_id(0) == 0) & (i == 0))
    def _():
        _init_tri(tri_ref)

    tri = tri_ref[...]
    causal = (lax.broadcasted_iota(jnp.int32, (tb, tb), 1)
              < lax.broadcasted_iota(jnp.int32, (tb, tb), 0))
    chains = [(b, h) for b in range(nqb) for h in range(heads)]

    def q_rows(b):
        return slice(b * tb, (b + 1) * tb)

    def head_cols(h):
        return slice(h * SB_HEAD_DIM, (h + 1) * SB_HEAD_DIM)

    def first_span(b, h, with_previous_block):
        hc = head_cols(h)
        qblock = i * nqb + b
        if with_previous_block:
            keys = pl.ds(pl.multiple_of((qblock - 1) * tb, tb), 2 * tb)
            valids = [None, causal]
        else:
            keys = pl.ds(pl.multiple_of(qblock * tb, tb), tb)
            valids = [causal]
        o, carry = _sb_span(q_ref[q_rows(b), hc], k_ref[keys, hc], v_ref[keys, hc], valids,
                            tri, None)
        slot = b * heads + h
        if older_only:
            o = jnp.zeros_like(o)
        o_ref[q_rows(b), hc] = o.astype(o_ref.dtype)
        _head_rows(acc_ref, slot, tb)[...] = o
        _head_rows(carry_ref, slot, tb)[...] = carry
        return carry

    def run(first_block_has_previous):
        lowest = None
        for b, h in chains:
            carry = first_span(b, h, first_block_has_previous or b > 0)
            if first_block_has_previous or b >= 2:
                lowest = carry if lowest is None else jnp.minimum(lowest, carry)
        if lowest is None:
            return

        @pl.when(jnp.min(lowest) < DEAD_LOG2)
        def _():
            for b, h in chains:
                hc = head_cols(h)
                slot = b * heads + h

                def load_kv(kb, hc=hc):
                    rows = pl.ds(pl.multiple_of(kb * tb, tb), tb)
                    return k_ref[rows, hc], v_ref[rows, hc]

                acc = _head_rows(acc_ref, slot, tb)
                _sb_older_blocks(q_ref[q_rows(b), hc], load_kv, i * nqb + b - 2, tri,
                                 _head_rows(carry_ref, slot, tb), acc)
                o_ref[q_rows(b), hc] = acc[...].astype(o_ref.dtype)

    @pl.when(i == 0)
    def _():
        run(False)

    @pl.when(i > 0)
    def _():
        run(True)


def _attn_prompt(q, k, v, *, older_only=False):
    s, width = q.shape
    hb = ATT_HEADS_PER_STEP
    tb = ATT_BLOCK
    tq = ATT_QBLOCKS_PER_STEP * tb
    assert s % tq == 0 and width % (hb * SB_HEAD_DIM) == 0
    wb = hb * SB_HEAD_DIM
    chains = ATT_QBLOCKS_PER_STEP * hb
    resident = pl.Buffered(1)
    vmem = 2 * s * wb * 2 + 2 * 2 * tq * wb * 2 + tb * tb * 2 + chains * 8 * tb * 2 * tb * 4
    return pl.pallas_call(
        functools.partial(_attn_prompt_kernel, older_only=older_only),
        grid=(width // wb, s // tq),
        in_specs=[
            pl.BlockSpec((tq, wb), lambda g, i: (i, g)),
            pl.BlockSpec((s, wb), lambda g, i: (0, g), pipeline_mode=resident),
            pl.BlockSpec((s, wb), lambda g, i: (0, g), pipeline_mode=resident),
        ],
        out_specs=pl.BlockSpec((tq, wb), lambda g, i: (i, g)),
        out_shape=jax.ShapeDtypeStruct((s, width), BF16),
        scratch_shapes=[
            pltpu.VMEM((tb, tb), BF16),
            pltpu.VMEM((chains * tb, 1), F32),
            pltpu.VMEM((chains * tb, SB_HEAD_DIM), F32),
        ],
        compiler_params=_compiler_params(("arbitrary", "arbitrary"), vmem),
        name="attn_prompt",
    )(q, k, v)


def _attn_sample_kernel(q_ref, kn_ref, vn_ref, ck_ref, cv_ref, o_ref,
                        tri_ref, carry_ref, acc_ref, *, heads):
    tq = q_ref.shape[0]
    past = ck_ref.shape[1] // heads
    tb = tri_ref.shape[1]
    head_rows = tb - tq
    nfull = (past - head_rows) // tb
    rem = past - head_rows - nfull * tb

    @pl.when(pl.program_id(0) == 0)
    def _():
        _init_tri(tri_ref)

    tri = tri_ref[...]
    row = lax.broadcasted_iota(jnp.int32, (tq, tb), 0)
    col = lax.broadcasted_iota(jnp.int32, (tq, tb), 1)

    def cache_rows(ref, h, start, size):
        return ref[0, pl.ds(start * heads + h, size, stride=heads), :].astype(BF16)

    lowest = None
    for h in range(heads):
        hc = slice(h * SB_HEAD_DIM, (h + 1) * SB_HEAD_DIM)
        k0 = jnp.concatenate([cache_rows(ck_ref, h, past - head_rows, head_rows), kn_ref[:, hc]],
                             axis=0)
        v0 = jnp.concatenate([cache_rows(cv_ref, h, past - head_rows, head_rows), vn_ref[:, hc]],
                             axis=0)
        o, carry = _sb_span(q_ref[:, hc], k0, v0, [col < row + head_rows], tri, None)
        o_ref[:, hc] = o.astype(o_ref.dtype)
        _head_rows(acc_ref, h, tq)[...] = o
        _head_rows(carry_ref, h, tq)[...] = carry
        lowest = carry if lowest is None else jnp.minimum(lowest, carry)

    @pl.when(jnp.min(lowest) < DEAD_LOG2)
    def _():
        for h in range(heads):
            hc = slice(h * SB_HEAD_DIM, (h + 1) * SB_HEAD_DIM)
            q = q_ref[:, hc]
            acc = _head_rows(acc_ref, h, tq)
            carry_h = _head_rows(carry_ref, h, tq)

            def load_kv(kb, h=h):
                start = rem + kb * tb
                return cache_rows(ck_ref, h, start, tb), cache_rows(cv_ref, h, start, tb)

            low = _sb_older_blocks(q, load_kv, nfull - 1, tri, carry_h, acc)

            if rem:
                @pl.when(low < DEAD_LOG2)
                def _():
                    o, _ = _sb_span(q, cache_rows(ck_ref, h, 0, tb), cache_rows(cv_ref, h, 0, tb),
                                    [col < rem], tri, carry_h[...])
                    acc[...] += o

            o_ref[:, hc] = acc[...].astype(o_ref.dtype)


def _attn_sample(q, kn, vn, cache_k, cache_v, batch, heads):
    rows, width = q.shape
    tq = rows // batch
    past = cache_k.shape[1] // heads
    tb = ATT_BLOCK
    assert tq < tb and past >= 2 * tb and tq % 16 == 0 and width == heads * SB_HEAD_DIM
    new_blk = lambda b: (b, 0)
    cache_blk = lambda b: (b, 0, 0)
    vmem = 2 * 2 * past * width * 4 + 2 * tb * tb * 2 + heads * 14 * tq * tb * 4
    return pl.pallas_call(
        functools.partial(_attn_sample_kernel, heads=heads),
        grid=(batch,),
        in_specs=[
            pl.BlockSpec((tq, width), new_blk),
            pl.BlockSpec((tq, width), new_blk),
            pl.BlockSpec((tq, width), new_blk),
            pl.BlockSpec((1, past * heads, SB_HEAD_DIM), cache_blk),
            pl.BlockSpec((1, past * heads, SB_HEAD_DIM), cache_blk),
        ],
        out_specs=pl.BlockSpec((tq, width), new_blk),
        out_shape=jax.ShapeDtypeStruct((rows, width), BF16),
        scratch_shapes=[
            pltpu.VMEM((tb, tb), BF16),
            pltpu.VMEM((heads * tq, 1), F32),
            pltpu.VMEM((heads * tq, SB_HEAD_DIM), F32),
        ],
        compiler_params=_compiler_params(("arbitrary",), vmem),
        name="attn_sample",
    )(q, kn, vn, cache_k, cache_v)


def _outproj_kernel(h_ref, osb_ref, om_ref, wa_ref, wb_ref, o_ref):
    o_ref[...] = (h_ref[...]
                  + jnp.dot(osb_ref[...], wa_ref[...], preferred_element_type=F32)
                  + jnp.dot(om_ref[...], wb_ref[...], preferred_element_type=F32))


def _outproj(h, osb, om, w_out):
    t, d = h.shape
    width = osb.shape[1]
    tm = min(ROW_TILE, t)
    assert t % tm == 0 and w_out.shape == (2 * width, d)
    vmem = 2 * 2 * tm * d * 4 + 2 * 2 * tm * width * 2 + 2 * 2 * width * d * 2
    return pl.pallas_call(
        _outproj_kernel,
        grid=(t // tm,),
        in_specs=[
            pl.BlockSpec((tm, d), lambda i: (i, 0)),
            pl.BlockSpec((tm, width), lambda i: (i, 0)),
            pl.BlockSpec((tm, width), lambda i: (i, 0)),
            pl.BlockSpec((width, d), lambda i: (0, 0)),
            pl.BlockSpec((width, d), lambda i: (1, 0)),
        ],
        out_specs=pl.BlockSpec((tm, d), lambda i: (i, 0)),
        out_shape=jax.ShapeDtypeStruct((t, d), F32),
        compiler_params=_compiler_params(("parallel",), vmem),
        name="outproj",
    )(h, osb, om, w_out, w_out)


def _mix_out_kernel(q_ref, kc_ref, vc_ref, kp_ref, vp_ref, h_ref, om_ref, w_ref,
                    o_ref, low_ref, tri_ref, osb0_ref, osb1_ref, *, nt):
    s = pl.program_id(0)
    tb = tri_ref.shape[0]
    width = q_ref.shape[1]
    heads = width // SB_HEAD_DIM
    nqb = q_ref.shape[0] // tb

    @pl.when(s == 0)
    def _():
        _init_tri(tri_ref)

    def project_columns(osb_ref, c, nchunks):
        cols = slice(c * (o_ref.shape[1] // nchunks), (c + 1) * (o_ref.shape[1] // nchunks))
        o_ref[:, cols] = (h_ref[:, cols]
                          + jnp.dot(osb_ref[...], w_ref[0:width, cols], preferred_element_type=F32)
                          + jnp.dot(om_ref[...], w_ref[width:, cols], preferred_element_type=F32))

    def attend(osb_ref, first_tile, project_from=None):
        tri = tri_ref[...]
        causal = (lax.broadcasted_iota(jnp.int32, (tb, tb), 1)
                  < lax.broadcasted_iota(jnp.int32, (tb, tb), 0))
        lowest = None
        nchunks = nqb * heads // 2
        for b in range(nqb):
            rows = slice(b * tb, (b + 1) * tb)
            for h in range(heads):
                if project_from is not None and (b * heads + h) % 2 == 0:
                    project_columns(project_from, (b * heads + h) // 2, nchunks)
                hc = slice(h * SB_HEAD_DIM, (h + 1) * SB_HEAD_DIM)
                if b > 0:
                    keys = slice((b - 1) * tb, (b + 1) * tb)
                    k, v, valids = kc_ref[keys, hc], vc_ref[keys, hc], [None, causal]
                elif first_tile:
                    k, v, valids = kc_ref[0:tb, hc], vc_ref[0:tb, hc], [causal]
                else:
                    k = jnp.concatenate([kp_ref[:, hc], kc_ref[0:tb, hc]], axis=0)
                    v = jnp.concatenate([vp_ref[:, hc], vc_ref[0:tb, hc]], axis=0)
                    valids = [None, causal]
                o, carry = _sb_span(q_ref[rows, hc], k, v, valids, tri, None)
                osb_ref[rows, hc] = o.astype(BF16)
                if not (first_tile and b < 2):
                    lowest = carry if lowest is None else jnp.minimum(lowest, carry)
        if lowest is None:
            low_ref[...] = jnp.full(low_ref.shape, DEAD_LOG2, F32)
        else:
            low_ref[...] = jnp.broadcast_to(jnp.min(lowest, axis=0, keepdims=True),
                                            low_ref.shape)

    def project(osb_ref):
        o_ref[...] = (h_ref[...]
                      + jnp.dot(osb_ref[...], w_ref[0:width], preferred_element_type=F32)
                      + jnp.dot(om_ref[...], w_ref[width:], preferred_element_type=F32))

    @pl.when(s == 0)
    def _():
        attend(osb0_ref, True)

    middle = (s > 0) & (s < nt)

    @pl.when(middle & (s % 2 == 1))
    def _():
        attend(osb1_ref, False, project_from=osb0_ref)

    @pl.when(middle & (s % 2 == 0))
    def _():
        attend(osb0_ref, False, project_from=osb1_ref)

    @pl.when(s == nt)
    def _():
        project(osb1_ref if (nt - 1) % 2 else osb0_ref)


def _mix_out(h, q, kb, vb, om, w_out):
    t, d = h.shape
    width = q.shape[1]
    tb = ATT_BLOCK
    tm = ATT_QBLOCKS_PER_STEP * tb
    assert t % tm == 0 and w_out.shape == (2 * width, d)
    nt = t // tm
    sub = tm // tb
    cur = lambda s: (jnp.minimum(s, nt - 1), 0)
    prev_blk = lambda s: (jnp.maximum(jnp.minimum(s, nt - 1) * sub - 1, 0), 0)
    done = lambda s: (jnp.maximum(s - 1, 0), 0)
    vmem = (2 * (3 * tm + 2 * tb) * width * 2 + 2 * 2 * tm * d * 4 + 2 * tm * width * 2
            + w_out.size * 2 + 2 * tm * width * 2 + 16 * 8 * tb * 2 * tb * 4 // 4)
    return pl.pallas_call(
        functools.partial(_mix_out_kernel, nt=nt),
        grid=(nt + 1,),
        in_specs=[
            pl.BlockSpec((tm, width), cur),
            pl.BlockSpec((tm, width), cur),
            pl.BlockSpec((tm, width), cur),
            pl.BlockSpec((tb, width), prev_blk),
            pl.BlockSpec((tb, width), prev_blk),
            pl.BlockSpec((tm, d), done),
            pl.BlockSpec((tm, width), done),
            pl.BlockSpec(w_out.shape, lambda s: (0, 0), pipeline_mode=pl.Buffered(1)),
        ],
        out_specs=[pl.BlockSpec((tm, d), done), pl.BlockSpec((8, LANES), cur)],
        out_shape=[jax.ShapeDtypeStruct((t, d), F32), jax.ShapeDtypeStruct((nt * 8, LANES), F32)],
        scratch_shapes=[pltpu.VMEM((tb, tb), BF16), pltpu.VMEM((tm, width), BF16),
                        pltpu.VMEM((tm, width), BF16)],
        compiler_params=_compiler_params(("arbitrary",), vmem),
        name="mix_out",
    )(q, kb, vb, kb, vb, h, om, w_out)


def _ple_kernel(h_ref, p_ref, gain_ref, wg_ref, wp_ref, fgain_ref, o_ref, *, final_norm):
    h = h_ref[...]
    n = (h * _rms_scale(h) * gain_ref[...]).astype(BF16)
    gate = jax.nn.sigmoid(jnp.dot(n, wg_ref[...], preferred_element_type=F32))
    proj = jnp.dot(p_ref[...].astype(BF16), wp_ref[...], preferred_element_type=F32)
    o = h + gate * proj
    if final_norm:
        o = o * _rms_scale(o) * fgain_ref[...]
    o_ref[...] = o


def _ple(h, p, gain, w_gate, w_proj, fgain, *, final_norm):
    t, d = h.shape
    pd = p.shape[1]
    tm = min(ROW_TILE, t)
    assert t % tm == 0
    vmem = 2 * 2 * tm * d * 4 + 2 * tm * pd * 4 + 2 * d * d * 2 + 2 * pd * d * 2 + 3 * tm * d * 4
    return pl.pallas_call(
        functools.partial(_ple_kernel, final_norm=final_norm),
        grid=(t // tm,),
        in_specs=[
            pl.BlockSpec((tm, d), lambda i: (i, 0)),
            pl.BlockSpec((tm, pd), lambda i: (i, 0)),
            pl.BlockSpec((1, d), lambda i: (0, 0)),
            pl.BlockSpec((d, d), lambda i: (0, 0)),
            pl.BlockSpec((pd, d), lambda i: (0, 0)),
            pl.BlockSpec((1, d), lambda i: (0, 0)),
        ],
        out_specs=pl.BlockSpec((tm, d), lambda i: (i, 0)),
        out_shape=jax.ShapeDtypeStruct((t, d), F32),
        compiler_params=_compiler_params(("parallel",), vmem),
        name="ple",
    )(h, p, gain.reshape(1, d), w_gate, w_proj, fgain.reshape(1, d))


def _gmlp_weights(ws, bs, seq_len):
    length = min(seq_len, MLP_CHUNK)
    assert MLP_CHUNK % length == 0
    reps = MLP_CHUNK // length
    if reps > 1:
        eye = jnp.eye(reps, dtype=ws.dtype)
        ws = jnp.einsum("ab,gij->gaibj", eye, ws[:, :length, :length]).reshape(
            ws.shape[0], MLP_CHUNK, MLP_CHUNK)
        bs = jnp.tile(bs[:, :length], (1, reps))
    return ws, jnp.transpose(bs)


def _layer(x, p, w, *, seq_len, final_gain, cache=None):
    rows = x.shape[0]
    w = dict(w)
    h, (w["ffn1_wg"], w["ffn1_wu"], w["ffn1_wd"]) = _ffn(
        x, w["ffn1_norm"], w["ffn1_wg"], w["ffn1_wu"], w["ffn1_wd"])
    ws, bst = _gmlp_weights(w["gmlp_ws"], w["gmlp_bs"], seq_len)
    outs = _inproj(h, w["mix_norm"], w["w_in"], w["gmlp_v_norm"], ws, bst,
                   emit_zn=cache is not None)
    q, k, v, kb, vb, om = outs[:6]
    if cache is None:
        h, low = _mix_out(h, q, kb, vb, om, w["w_out"])

        def finish_older_keys(h):
            older = _attn_prompt(q, kb, vb, older_only=True)
            return _outproj(h, older, jnp.zeros_like(om), w["w_out"])

        h = lax.cond(jnp.any(low < DEAD_LOG2), finish_older_keys, lambda h: h, h)
        zn = None
    else:
        osb = _attn_sample(q, kb, vb, cache[0], cache[1], rows // seq_len,
                           q.shape[1] // SB_HEAD_DIM)
        zn = outs[6]
        h = _outproj(h, osb, om, w["w_out"])
    h, (w["ffn2_wg"], w["ffn2_wu"], w["ffn2_wd"]) = _ffn(
        h, w["ffn2_norm"], w["ffn2_wg"], w["ffn2_wu"], w["ffn2_wd"])
    h = _ple(h, p, w["ple_norm"], w["ple_w_gate"], w["ple_w_proj"],
             final_gain if final_gain is not None else w["ple_norm"],
             final_norm=final_gain is not None)
    return h, k, v, zn, w


_PRECAST_WEIGHTS = ("w_in", "w_out", "ple_w_gate", "ple_w_proj")


def kernel(x_prompt, x_sample, cache_k, cache_v, p_prompt, p_sample, ffn1_norm, ffn1_wg, ffn1_wu, ffn1_wd, mix_norm, w_in, gmlp_v_norm, gmlp_ws, gmlp_bs, w_out, ffn2_norm, ffn2_wg, ffn2_wu, ffn2_wd, ple_norm, ple_w_gate, ple_w_proj, final_norm):
    weights = dict(ffn1_norm=ffn1_norm, ffn1_wg=ffn1_wg, ffn1_wu=ffn1_wu, ffn1_wd=ffn1_wd,
                   mix_norm=mix_norm, w_in=w_in, gmlp_v_norm=gmlp_v_norm, gmlp_ws=gmlp_ws,
                   gmlp_bs=gmlp_bs, w_out=w_out, ffn2_norm=ffn2_norm, ffn2_wg=ffn2_wg,
                   ffn2_wu=ffn2_wu, ffn2_wd=ffn2_wd, ple_norm=ple_norm,
                   ple_w_gate=ple_w_gate, ple_w_proj=ple_w_proj)
    depth = w_in.shape[0]
    bp, sp, d = x_prompt.shape
    bs, ss, _ = x_sample.shape
    assert bp == 1, "prompt attention handles one sequence"
    heads = cache_k.shape[3]
    width = heads * SB_HEAD_DIM

    hp = x_prompt.reshape(bp * sp, d)
    hs = x_sample.reshape(bs * ss, d)
    kp_l, vp_l, ks_l, vs_l, zs_l = [], [], [], [], []
    for i in range(depth):
        w = {name: (val[i].astype(BF16) if name in _PRECAST_WEIGHTS else val[i])
             for name, val in weights.items()}
        fgain = final_norm if i == depth - 1 else None
        cache = (cache_k[i].reshape(bs, -1, SB_HEAD_DIM), cache_v[i].reshape(bs, -1, SB_HEAD_DIM))
        hs, k_s, v_s, z_s, w = _layer(hs, p_sample[i].reshape(bs * ss, -1), w, seq_len=ss,
                                      final_gain=fgain, cache=cache)
        hp, kp, vp, _, _ = _layer(hp, p_prompt[i].reshape(bp * sp, -1), w, seq_len=sp,
                                  final_gain=fgain)
        kp_l.append(kp.reshape(bp, sp, heads, SB_HEAD_DIM))
        vp_l.append(vp.reshape(bp, sp, heads, SB_HEAD_DIM))
        ks_l.append(k_s.reshape(bs, ss, heads, SB_HEAD_DIM))
        vs_l.append(v_s.reshape(bs, ss, heads, SB_HEAD_DIM))
        zs_l.append(z_s.reshape(bs, ss, -1, MLP_GROUP_DIM))
    return (hp.reshape(bp, sp, d), hs.reshape(bs, ss, d),
            jnp.stack(kp_l), jnp.stack(vp_l), jnp.stack(ks_l), jnp.stack(vs_l), jnp.stack(zs_l))
```

```python
import functools

import jax
import jax.numpy as jnp
from jax import lax
from jax.experimental import pallas as pl
from jax.experimental.pallas import tpu as pltpu

F32 = jnp.float32
BF16 = jnp.bfloat16

EPS = 1e-6
SB_HEAD_DIM = 128
MLP_GROUP_DIM = 128
MLP_CHUNK = 128

V7X_VMEM_BYTES = 64 * 1024 * 1024
VMEM_REQUEST_CAP = V7X_VMEM_BYTES - 6 * 1024 * 1024
LANES = 128

DEAD_LOG2 = 152.0
LOG2_E = 1.4426950408889634

ROW_TILE = 512
FFN_ROW_TILE = 1024
FFN_COL_TILE = 512
FFN_CAST_COL_TILE = 256
ATT_BLOCK = 256
ATT_HEADS_PER_STEP = 4
ATT_QBLOCKS_PER_STEP = 2


def _compiler_params(semantics, vmem_bytes):
    limit = min(int(vmem_bytes * 1.2) + (4 << 20), VMEM_REQUEST_CAP)
    return pltpu.CompilerParams(dimension_semantics=semantics, vmem_limit_bytes=limit)


def _rms_scale(x):
    return lax.rsqrt(jnp.mean(x * x, axis=-1, keepdims=True) + EPS)


def _bf16_weight(w_ref, cast_ref):
    if cast_ref is None:
        return w_ref[...]
    w = w_ref[...].astype(BF16)
    cast_ref[...] = w
    return w


def _ffn_kernel(x_ref, gain_ref, wg_ref, wu_ref, wd_ref, o_ref, *rest, nt, nf, cast_weights):
    if cast_weights:
        wg_cast, wu_cast, wd_cast, n_ref, a0_ref, a1_ref = rest
    else:
        n_ref, a0_ref, a1_ref = rest
        wg_cast = wu_cast = wd_cast = None
    j = pl.program_id(1)

    def up(a_ref):
        n = n_ref[...]
        g = jnp.dot(n, _bf16_weight(wg_ref, wg_cast), preferred_element_type=F32)
        u = jnp.dot(n, _bf16_weight(wu_ref, wu_cast), preferred_element_type=F32)
        a_ref[...] = (0.5 * g * jax.nn.sigmoid(g) * u).astype(BF16)

    def down(a_ref):
        o_ref[...] += jnp.dot(a_ref[...], _bf16_weight(wd_ref, wd_cast),
                              preferred_element_type=F32)

    i = pl.program_id(0)

    def normalize():
        x = x_ref[...]
        n_ref[...] = (x * _rms_scale(x) * gain_ref[...]).astype(BF16)

    @pl.when((j == 0) & (i == 0))
    def _():
        normalize()

    @pl.when(j == 0)
    def _():
        o_ref[...] = x_ref[...]
        up(a0_ref)

    middle = (j > 0) & (j < nf)

    @pl.when(middle & (j % 2 == 1))
    def _():
        down(a0_ref)
        up(a1_ref)

    @pl.when(middle & (j % 2 == 0))
    def _():
        down(a1_ref)
        up(a0_ref)

    last_a = a1_ref if (nf - 1) % 2 else a0_ref

    @pl.when((j == nf) & (i < nt - 1))
    def _():
        down(last_a)
        normalize()

    @pl.when((j == nf) & (i == nt - 1))
    def _():
        down(last_a)


def _ffn(x, gain, wg, wu, wd):
    t, d = x.shape
    f = wg.shape[1]
    cast_weights = wg.dtype != BF16
    tm = min(FFN_ROW_TILE, t)
    tf = FFN_CAST_COL_TILE if cast_weights else FFN_COL_TILE
    assert t % tm == 0 and f % tf == 0
    assert not cast_weights or t == tm
    nf = f // tf
    wbytes = jnp.dtype(wg.dtype).itemsize
    x_bufs = 1 if cast_weights else 2
    vmem = ((x_bufs + 2) * tm * d * 4 + tm * d * 2 + 2 * 3 * d * tf * wbytes
            + 2 * tm * tf * 2 + 2 * tm * tf * 4)
    up_blk = lambda i, j: (0, jnp.minimum(j, nf - 1))
    down_blk = lambda i, j: (jnp.maximum(j - 1, 0), 0)
    row_blk = lambda i, j: (i, 0)
    out_specs = [pl.BlockSpec((tm, d), row_blk)]
    out_shape = [jax.ShapeDtypeStruct((t, d), F32)]
    if cast_weights:
        vmem += 2 * 3 * d * tf * 2
        out_specs += [pl.BlockSpec((d, tf), up_blk), pl.BlockSpec((d, tf), up_blk),
                      pl.BlockSpec((tf, d), down_blk)]
        out_shape += [jax.ShapeDtypeStruct(w.shape, BF16) for w in (wg, wu, wd)]
    nt = t // tm
    x_blk = lambda i, j: (jnp.minimum(jnp.where(j == nf, i + 1, i), nt - 1), 0)
    x_spec = (pl.BlockSpec((tm, d), x_blk, pipeline_mode=pl.Buffered(1)) if cast_weights
              else pl.BlockSpec((tm, d), x_blk))
    outs = pl.pallas_call(
        functools.partial(_ffn_kernel, nt=nt, nf=nf, cast_weights=cast_weights),
        grid=(nt, nf + 1),
        in_specs=[
            x_spec,
            pl.BlockSpec((1, d), lambda i, j: (0, 0)),
            pl.BlockSpec((d, tf), up_blk),
            pl.BlockSpec((d, tf), up_blk),
            pl.BlockSpec((tf, d), down_blk),
        ],
        out_specs=out_specs,
        out_shape=out_shape,
        scratch_shapes=[pltpu.VMEM((tm, d), BF16), pltpu.VMEM((tm, tf), BF16),
                        pltpu.VMEM((tm, tf), BF16)],
        compiler_params=_compiler_params(("arbitrary", "arbitrary"), vmem),
        name="ffn",
    )(x, gain.reshape(1, d), wg, wu, wd)
    return (outs[0], tuple(outs[1:])) if cast_weights else (outs[0], (wg, wu, wd))


INPROJ_GROUPS = dict(q=0, k=1, v=2, u=3, z=4)


def _gmlp_mix(u, zn, ws_ref, bst_ref, om_ref):
    znb = zn.astype(BF16)
    row = lax.broadcasted_iota(jnp.int32, (MLP_CHUNK, MLP_CHUNK), 0)
    col = lax.broadcasted_iota(jnp.int32, (MLP_CHUNK, MLP_CHUNK), 1)
    lower = col <= row
    for g in range(ws_ref.shape[0]):
        w = jnp.where(lower, ws_ref[g], 0.0).astype(BF16)
        bias = bst_ref[:, g:g + 1]
        cs = slice(g * MLP_GROUP_DIM, (g + 1) * MLP_GROUP_DIM)
        for c in range(znb.shape[0] // MLP_CHUNK):
            rs = slice(c * MLP_CHUNK, (c + 1) * MLP_CHUNK)
            mixed = jnp.dot(w, znb[rs, cs], preferred_element_type=F32) + bias
            om_ref[rs, cs] = (u[rs, cs] * mixed).astype(BF16)


def _inproj_kernel(h_ref, gain_ref, w_ref, vgain_ref, ws_ref, bst_ref,
                   q_ref, k_ref, v_ref, kb_ref, vb_ref, om_ref, *zn_out):
    width = q_ref.shape[1]
    h = h_ref[...]
    n = (h * _rms_scale(h) * gain_ref[...]).astype(BF16)

    def project(name):
        g = INPROJ_GROUPS[name]
        return jnp.dot(n, w_ref[:, g * width:(g + 1) * width], preferred_element_type=F32)

    u = jax.nn.gelu(project("u"), approximate=True)
    gz = jax.nn.gelu(project("z"), approximate=True)
    zn = gz * _rms_scale(gz) * vgain_ref[...]
    if zn_out:
        zn_out[0][...] = zn
    _gmlp_mix(u, zn, ws_ref, bst_ref, om_ref)
    q_ref[...] = (project("q") * (SB_HEAD_DIM ** -0.5 * LOG2_E)).astype(BF16)
    k = project("k")
    k_ref[...] = k
    kb_ref[...] = k.astype(BF16)
    v = project("v")
    v_ref[...] = v
    vb_ref[...] = v.astype(BF16)


def _inproj(h, gain, w_in, vgain, ws, bst, *, emit_zn):
    t, d = h.shape
    width = w_in.shape[1] // len(INPROJ_GROUPS)
    tm = min(ROW_TILE, t)
    assert t % tm == 0 and tm % MLP_CHUNK == 0 and w_in.shape[1] == len(INPROJ_GROUPS) * width
    groups = ws.shape[0]
    row_blk = lambda i: (i, 0)
    const2 = lambda i: (0, 0)
    out_shape = [
        jax.ShapeDtypeStruct((t, width), BF16),
        jax.ShapeDtypeStruct((t, width), F32),
        jax.ShapeDtypeStruct((t, width), F32),
        jax.ShapeDtypeStruct((t, width), BF16),
        jax.ShapeDtypeStruct((t, width), BF16),
        jax.ShapeDtypeStruct((t, width), BF16),
    ]
    if emit_zn:
        out_shape.append(jax.ShapeDtypeStruct((t, width), F32))
    out_specs = [pl.BlockSpec((tm, width), row_blk) for _ in out_shape]
    out_bytes = sum(tm * width * jnp.dtype(s.dtype).itemsize for s in out_shape)
    vmem = (2 * tm * d * 4 + w_in.size * 2 + 2 * out_bytes + tm * d * 2 + 6 * tm * width * 4)
    return pl.pallas_call(
        _inproj_kernel,
        grid=(t // tm,),
        in_specs=[
            pl.BlockSpec((tm, d), row_blk),
            pl.BlockSpec((1, d), const2),
            pl.BlockSpec(w_in.shape, const2, pipeline_mode=pl.Buffered(1)),
            pl.BlockSpec((1, width), const2),
            pl.BlockSpec((groups, MLP_CHUNK, MLP_CHUNK), lambda i: (0, 0, 0)),
            pl.BlockSpec((MLP_CHUNK, groups), const2),
        ],
        out_specs=out_specs,
        out_shape=out_shape,
        compiler_params=_compiler_params(("parallel",), vmem),
        name="inproj",
    )(h, gain.reshape(1, d), w_in, vgain.reshape(1, width), ws, bst)


def _init_tri(tri_ref):
    j = lax.broadcasted_iota(jnp.int32, tri_ref.shape, 0)
    s = lax.broadcasted_iota(jnp.int32, tri_ref.shape, 1)
    tri_ref[...] = jnp.where(j > s, -1.0, 0.0).astype(BF16)


def _sb_span(q, k, v, valids, tri, carry):
    tb = tri.shape[1]
    u = lax.dot_general(q, k, (((1,), (1,)), ((), ())), preferred_element_type=F32)
    s = jnp.maximum(u, 0.0) + jnp.log2(1.0 + jnp.exp2(jnp.minimum(u, -u)))
    weights = [None] * len(valids)
    for b in reversed(range(len(valids))):
        cols = slice(b * tb, (b + 1) * tb)
        sb = s[:, cols]
        if valids[b] is not None:
            sb = jnp.where(valids[b], sb, 0.0)
        newer = jnp.dot(sb.astype(BF16), tri, preferred_element_type=F32)
        log2_a = u[:, cols] - sb + newer
        total = jnp.sum(sb, axis=1, keepdims=True)
        if carry is None:
            carry = total
        else:
            log2_a = log2_a - carry
            carry = carry + total
        a = jnp.exp2(log2_a)
        if valids[b] is not None:
            a = jnp.where(valids[b], a, 0.0)
        weights[b] = a.astype(BF16)
    a = weights[0] if len(weights) == 1 else jnp.concatenate(weights, axis=1)
    return jnp.dot(a, v, preferred_element_type=F32), carry


def _head_rows(ref, h, rows):
    return ref.at[pl.ds(h * rows, rows)]


def _sb_older_blocks(q, load_kv, first_kb, tri, carry_ref, acc_ref):
    def cond(state):
        kb, low = state
        return (kb >= 0) & (low < DEAD_LOG2)

    def body(state):
        kb, _ = state
        k, v = load_kv(kb)
        o, carry = _sb_span(q, k, v, [None], tri, carry_ref[...])
        acc_ref[...] += o
        carry_ref[...] = carry
        return kb - 1, jnp.min(carry)

    return lax.while_loop(cond, body, (first_kb, jnp.min(carry_ref[...])))[1]


def _attn_prompt_kernel(q_ref, k_ref, v_ref, o_ref, tri_ref, carry_ref, acc_ref, *, older_only):
    i = pl.program_id(1)
    tb = tri_ref.shape[0]
    nqb = q_ref.shape[0] // tb
    heads = q_ref.shape[1] // SB_HEAD_DIM

    @pl.when((pl.program_id(0) == 0) & (i == 0))
    def _():
        _init_tri(tri_ref)

    tri = tri_ref[...]
    causal = (lax.broadcasted_iota(jnp.int32, (tb, tb), 1)
              < lax.broadcasted_iota(jnp.int32, (tb, tb), 0))
    chains = [(b, h) for b in range(nqb) for h in range(heads)]

    def q_rows(b):
        return slice(b * tb, (b + 1) * tb)

    def head_cols(h):
        return slice(h * SB_HEAD_DIM, (h + 1) * SB_HEAD_DIM)

    def first_span(b, h, with_previous_block):
        hc = head_cols(h)
        qblock = i * nqb + b
        if with_previous_block:
            keys = pl.ds(pl.multiple_of((qblock - 1) * tb, tb), 2 * tb)
            valids = [None, causal]
        else:
            keys = pl.ds(pl.multiple_of(qblock * tb, tb), tb)
            valids = [causal]
        o, carry = _sb_span(q_ref[q_rows(b), hc], k_ref[keys, hc], v_ref[keys, hc], valids,
                            tri, None)
        slot = b * heads + h
        if older_only:
            o = jnp.zeros_like(o)
        o_ref[q_rows(b), hc] = o.astype(o_ref.dtype)
        _head_rows(acc_ref, slot, tb)[...] = o
        _head_rows(carry_ref, slot, tb)[...] = carry
        return carry

    def run(first_block_has_previous):
        lowest = None
        for b, h in chains:
            carry = first_span(b, h, first_block_has_previous or b > 0)
            if first_block_has_previous or b >= 2:
                lowest = carry if lowest is None else jnp.minimum(lowest, carry)
        if lowest is None:
            return

        @pl.when(jnp.min(lowest) < DEAD_LOG2)
        def _():
            for b, h in chains:
                hc = head_cols(h)
                slot = b * heads + h

                def load_kv(kb, hc=hc):
                    rows = pl.ds(pl.multiple_of(kb * tb, tb), tb)
                    return k_ref[rows, hc], v_ref[rows, hc]

                acc = _head_rows(acc_ref, slot, tb)
                _sb_older_blocks(q_ref[q_rows(b), hc], load_kv, i * nqb + b - 2, tri,
                                 _head_rows(carry_ref, slot, tb), acc)
                o_ref[q_rows(b), hc] = acc[...].astype(o_ref.dtype)

    @pl.when(i == 0)
    def _():
        run(False)

    @pl.when(i > 0)
    def _():
        run(True)


def _attn_prompt(q, k, v, *, older_only=False):
    s, width = q.shape
    hb = ATT_HEADS_PER_STEP
    tb = ATT_BLOCK
    tq = ATT_QBLOCKS_PER_STEP * tb
    assert s % tq == 0 and width % (hb * SB_HEAD_DIM) == 0
    wb = hb * SB_HEAD_DIM
    chains = ATT_QBLOCKS_PER_STEP * hb
    resident = pl.Buffered(1)
    vmem = 2 * s * wb * 2 + 2 * 2 * tq * wb * 2 + tb * tb * 2 + chains * 8 * tb * 2 * tb * 4
    return pl.pallas_call(
        functools.partial(_attn_prompt_kernel, older_only=older_only),
        grid=(width // wb, s // tq),
        in_specs=[
            pl.BlockSpec((tq, wb), lambda g, i: (i, g)),
            pl.BlockSpec((s, wb), lambda g, i: (0, g), pipeline_mode=resident),
            pl.BlockSpec((s, wb), lambda g, i: (0, g), pipeline_mode=resident),
        ],
        out_specs=pl.BlockSpec((tq, wb), lambda g, i: (i, g)),
        out_shape=jax.ShapeDtypeStruct((s, width), BF16),
        scratch_shapes=[
            pltpu.VMEM((tb, tb), BF16),
            pltpu.VMEM((chains * tb, 1), F32),
            pltpu.VMEM((chains * tb, SB_HEAD_DIM), F32),
        ],
        compiler_params=_compiler_params(("arbitrary", "arbitrary"), vmem),
        name="attn_prompt",
    )(q, k, v)


def _attn_sample_kernel(q_ref, kn_ref, vn_ref, ck_ref, cv_ref, o_ref, *rest, heads, older_only):
    if older_only:
        tri_ref, carry_ref, acc_ref = rest
    else:
        low_ref, tri_ref = rest
    tq = q_ref.shape[0]
    held = ck_ref.shape[1] // heads
    tb = tri_ref.shape[1]
    head_rows = tb - tq
    first = held - head_rows

    @pl.when(pl.program_id(0) == 0)
    def _():
        _init_tri(tri_ref)

    tri = tri_ref[...]
    row = lax.broadcasted_iota(jnp.int32, (tq, tb), 0)
    col = lax.broadcasted_iota(jnp.int32, (tq, tb), 1)

    def cache_rows(ref, h, start, size):
        return ref[0, pl.ds(start * heads + h, size, stride=heads), :].astype(BF16)

    lowest = None
    for h in range(heads):
        hc = slice(h * SB_HEAD_DIM, (h + 1) * SB_HEAD_DIM)
        k0 = jnp.concatenate([cache_rows(ck_ref, h, first, head_rows), kn_ref[:, hc]], axis=0)
        v0 = jnp.concatenate([cache_rows(cv_ref, h, first, head_rows), vn_ref[:, hc]], axis=0)
        o, carry = _sb_span(q_ref[:, hc], k0, v0, [col < row + head_rows], tri, None)
        if older_only:
            o = jnp.zeros_like(o)
            _head_rows(acc_ref, h, tq)[...] = o
            _head_rows(carry_ref, h, tq)[...] = carry
        o_ref[:, hc] = o.astype(o_ref.dtype)
        lowest = carry if lowest is None else jnp.minimum(lowest, carry)

    if not older_only:
        low_ref[...] = jnp.broadcast_to(jnp.min(lowest, axis=0, keepdims=True), low_ref.shape)
        return

    nfull = first // tb
    rem = first - nfull * tb

    @pl.when(jnp.min(lowest) < DEAD_LOG2)
    def _():
        for h in range(heads):
            hc = slice(h * SB_HEAD_DIM, (h + 1) * SB_HEAD_DIM)
            q = q_ref[:, hc]
            acc = _head_rows(acc_ref, h, tq)
            carry_h = _head_rows(carry_ref, h, tq)

            def load_kv(kb, h=h):
                start = rem + kb * tb
                return cache_rows(ck_ref, h, start, tb), cache_rows(cv_ref, h, start, tb)

            low = _sb_older_blocks(q, load_kv, nfull - 1, tri, carry_h, acc)

            if rem:
                @pl.when(low < DEAD_LOG2)
                def _():
                    o, _ = _sb_span(q, cache_rows(ck_ref, h, 0, tb), cache_rows(cv_ref, h, 0, tb),
                                    [col < rem], tri, carry_h[...])
                    acc[...] += o

            o_ref[:, hc] = acc[...].astype(o_ref.dtype)


def _attn_sample(q, kn, vn, cache_k, cache_v, batch, heads):
    rows, width = q.shape
    tq = rows // batch
    past = cache_k.shape[1] // heads
    tb = ATT_BLOCK
    assert tq < tb and past >= 2 * tb and past % tb == 0 and tq % 16 == 0
    assert width == heads * SB_HEAD_DIM
    new_blk = lambda b: (b, 0)
    new_specs = [pl.BlockSpec((tq, width), new_blk)] * 3

    def call(older_only):
        held = past if older_only else tb
        cache_spec = pl.BlockSpec((1, held * heads, SB_HEAD_DIM),
                                  lambda b: (b, past // held - 1, 0))
        out_specs = [pl.BlockSpec((tq, width), new_blk)]
        out_shape = [jax.ShapeDtypeStruct((rows, width), BF16)]
        scratch = [pltpu.VMEM((tb, tb), BF16)]
        if older_only:
            scratch += [pltpu.VMEM((heads * tq, 1), F32),
                        pltpu.VMEM((heads * tq, SB_HEAD_DIM), F32)]
        else:
            out_specs.append(pl.BlockSpec((8, LANES), new_blk))
            out_shape.append(jax.ShapeDtypeStruct((batch * 8, LANES), F32))
        vmem = 2 * 2 * held * width * 4 + tb * tb * 2 + heads * 14 * tq * tb * 4
        return pl.pallas_call(
            functools.partial(_attn_sample_kernel, heads=heads, older_only=older_only),
            grid=(batch,),
            in_specs=new_specs + [cache_spec, cache_spec],
            out_specs=out_specs,
            out_shape=out_shape,
            scratch_shapes=scratch,
            compiler_params=_compiler_params(("arbitrary",), vmem),
            name="attn_sample_older" if older_only else "attn_sample",
        )(q, kn, vn, cache_k, cache_v)

    o, low = call(False)

    def add_older_rows(o):
        older = call(True)[0]
        return (o.astype(F32) + older.astype(F32)).astype(BF16)

    return lax.cond(jnp.any(low < DEAD_LOG2), add_older_rows, lambda o: o, o)


def _outproj_kernel(h_ref, osb_ref, om_ref, wa_ref, wb_ref, o_ref):
    o_ref[...] = (h_ref[...]
                  + jnp.dot(osb_ref[...], wa_ref[...], preferred_element_type=F32)
                  + jnp.dot(om_ref[...], wb_ref[...], preferred_element_type=F32))


def _outproj(h, osb, om, w_out):
    t, d = h.shape
    width = osb.shape[1]
    tm = min(ROW_TILE, t)
    assert t % tm == 0 and w_out.shape == (2 * width, d)
    vmem = 2 * 2 * tm * d * 4 + 2 * 2 * tm * width * 2 + 2 * 2 * width * d * 2
    return pl.pallas_call(
        _outproj_kernel,
        grid=(t // tm,),
        in_specs=[
            pl.BlockSpec((tm, d), lambda i: (i, 0)),
            pl.BlockSpec((tm, width), lambda i: (i, 0)),
            pl.BlockSpec((tm, width), lambda i: (i, 0)),
            pl.BlockSpec((width, d), lambda i: (0, 0)),
            pl.BlockSpec((width, d), lambda i: (1, 0)),
        ],
        out_specs=pl.BlockSpec((tm, d), lambda i: (i, 0)),
        out_shape=jax.ShapeDtypeStruct((t, d), F32),
        compiler_params=_compiler_params(("parallel",), vmem),
        name="outproj",
    )(h, osb, om, w_out, w_out)


def _mix_out_kernel(q_ref, kc_ref, vc_ref, kp_ref, vp_ref, h_ref, om_ref, w_ref,
                    o_ref, low_ref, tri_ref, osb0_ref, osb1_ref, *, nt):
    s = pl.program_id(0)
    tb = tri_ref.shape[0]
    width = q_ref.shape[1]
    heads = width // SB_HEAD_DIM
    nqb = q_ref.shape[0] // tb

    @pl.when(s == 0)
    def _():
        _init_tri(tri_ref)

    def project_columns(osb_ref, c, nchunks):
        cols = slice(c * (o_ref.shape[1] // nchunks), (c + 1) * (o_ref.shape[1] // nchunks))
        o_ref[:, cols] = (h_ref[:, cols]
                          + jnp.dot(osb_ref[...], w_ref[0:width, cols], preferred_element_type=F32)
                          + jnp.dot(om_ref[...], w_ref[width:, cols], preferred_element_type=F32))

    def attend(osb_ref, first_tile, project_from=None):
        tri = tri_ref[...]
        causal = (lax.broadcasted_iota(jnp.int32, (tb, tb), 1)
                  < lax.broadcasted_iota(jnp.int32, (tb, tb), 0))
        lowest = None
        nchunks = nqb * heads // 2
        for b in range(nqb):
            rows = slice(b * tb, (b + 1) * tb)
            for h in range(heads):
                if project_from is not None and (b * heads + h) % 2 == 0:
                    project_columns(project_from, (b * heads + h) // 2, nchunks)
                hc = slice(h * SB_HEAD_DIM, (h + 1) * SB_HEAD_DIM)
                if b > 0:
                    keys = slice((b - 1) * tb, (b + 1) * tb)
                    k, v, valids = kc_ref[keys, hc], vc_ref[keys, hc], [None, causal]
                elif first_tile:
                    k, v, valids = kc_ref[0:tb, hc], vc_ref[0:tb, hc], [causal]
                else:
                    k = jnp.concatenate([kp_ref[:, hc], kc_ref[0:tb, hc]], axis=0)
                    v = jnp.concatenate([vp_ref[:, hc], vc_ref[0:tb, hc]], axis=0)
                    valids = [None, causal]
                o, carry = _sb_span(q_ref[rows, hc], k, v, valids, tri, None)
                osb_ref[rows, hc] = o.astype(BF16)
                if not (first_tile and b < 2):
                    lowest = carry if lowest is None else jnp.minimum(lowest, carry)
        if lowest is None:
            low_ref[...] = jnp.full(low_ref.shape, DEAD_LOG2, F32)
        else:
            low_ref[...] = jnp.broadcast_to(jnp.min(lowest, axis=0, keepdims=True),
                                            low_ref.shape)

    def project(osb_ref):
        o_ref[...] = (h_ref[...]
                      + jnp.dot(osb_ref[...], w_ref[0:width], preferred_element_type=F32)
                      + jnp.dot(om_ref[...], w_ref[width:], preferred_element_type=F32))

    @pl.when(s == 0)
    def _():
        attend(osb0_ref, True)

    middle = (s > 0) & (s < nt)

    @pl.when(middle & (s % 2 == 1))
    def _():
        attend(osb1_ref, False, project_from=osb0_ref)

    @pl.when(middle & (s % 2 == 0))
    def _():
        attend(osb0_ref, False, project_from=osb1_ref)

    @pl.when(s == nt)
    def _():
        project(osb1_ref if (nt - 1) % 2 else osb0_ref)


def _mix_out(h, q, kb, vb, om, w_out):
    t, d = h.shape
    width = q.shape[1]
    tb = ATT_BLOCK
    tm = ATT_QBLOCKS_PER_STEP * tb
    assert t % tm == 0 and w_out.shape == (2 * width, d)
    nt = t // tm
    sub = tm // tb
    cur = lambda s: (jnp.minimum(s, nt - 1), 0)
    prev_blk = lambda s: (jnp.maximum(jnp.minimum(s, nt - 1) * sub - 1, 0), 0)
    done = lambda s: (jnp.maximum(s - 1, 0), 0)
    vmem = (2 * (3 * tm + 2 * tb) * width * 2 + 2 * 2 * tm * d * 4 + 2 * tm * width * 2
            + w_out.size * 2 + 2 * tm * width * 2 + 16 * 8 * tb * 2 * tb * 4 // 4)
    return pl.pallas_call(
        functools.partial(_mix_out_kernel, nt=nt),
        grid=(nt + 1,),
        in_specs=[
            pl.BlockSpec((tm, width), cur),
            pl.BlockSpec((tm, width), cur),
            pl.BlockSpec((tm, width), cur),
            pl.BlockSpec((tb, width), prev_blk),
            pl.BlockSpec((tb, width), prev_blk),
            pl.BlockSpec((tm, d), done),
            pl.BlockSpec((tm, width), done),
            pl.BlockSpec(w_out.shape, lambda s: (0, 0), pipeline_mode=pl.Buffered(1)),
        ],
        out_specs=[pl.BlockSpec((tm, d), done), pl.BlockSpec((8, LANES), cur)],
        out_shape=[jax.ShapeDtypeStruct((t, d), F32), jax.ShapeDtypeStruct((nt * 8, LANES), F32)],
        scratch_shapes=[pltpu.VMEM((tb, tb), BF16), pltpu.VMEM((tm, width), BF16),
                        pltpu.VMEM((tm, width), BF16)],
        compiler_params=_compiler_params(("arbitrary",), vmem),
        name="mix_out",
    )(q, kb, vb, kb, vb, h, om, w_out)


def _ple_kernel(h_ref, p_ref, gain_ref, wg_ref, wp_ref, fgain_ref, o_ref, *, final_norm):
    h = h_ref[...]
    n = (h * _rms_scale(h) * gain_ref[...]).astype(BF16)
    gate = jax.nn.sigmoid(jnp.dot(n, wg_ref[...], preferred_element_type=F32))
    proj = jnp.dot(p_ref[...].astype(BF16), wp_ref[...], preferred_element_type=F32)
    o = h + gate * proj
    if final_norm:
        o = o * _rms_scale(o) * fgain_ref[...]
    o_ref[...] = o


def _ple(h, p, gain, w_gate, w_proj, fgain, *, final_norm):
    t, d = h.shape
    pd = p.shape[1]
    tm = min(ROW_TILE, t)
    assert t % tm == 0
    vmem = 2 * 2 * tm * d * 4 + 2 * tm * pd * 4 + 2 * d * d * 2 + 2 * pd * d * 2 + 3 * tm * d * 4
    return pl.pallas_call(
        functools.partial(_ple_kernel, final_norm=final_norm),
        grid=(t // tm,),
        in_specs=[
            pl.BlockSpec((tm, d), lambda i: (i, 0)),
            pl.BlockSpec((tm, pd), lambda i: (i, 0)),
            pl.BlockSpec((1, d), lambda i: (0, 0)),
            pl.BlockSpec((d, d), lambda i: (0, 0)),
            pl.BlockSpec((pd, d), lambda i: (0, 0)),
            pl.BlockSpec((1, d), lambda i: (0, 0)),
        ],
        out_specs=pl.BlockSpec((tm, d), lambda i: (i, 0)),
        out_shape=jax.ShapeDtypeStruct((t, d), F32),
        compiler_params=_compiler_params(("parallel",), vmem),
        name="ple",
    )(h, p, gain.reshape(1, d), w_gate, w_proj, fgain.reshape(1, d))


def _gmlp_weights(ws, bs, seq_len):
    length = min(seq_len, MLP_CHUNK)
    assert MLP_CHUNK % length == 0
    reps = MLP_CHUNK // length
    if reps > 1:
        eye = jnp.eye(reps, dtype=ws.dtype)
        ws = jnp.einsum("ab,gij->gaibj", eye, ws[:, :length, :length]).reshape(
            ws.shape[0], MLP_CHUNK, MLP_CHUNK)
        bs = jnp.tile(bs[:, :length], (1, reps))
    return ws, jnp.transpose(bs)


def _layer(x, p, w, *, seq_len, final_gain, cache=None):
    rows = x.shape[0]
    w = dict(w)
    h, (w["ffn1_wg"], w["ffn1_wu"], w["ffn1_wd"]) = _ffn(
        x, w["ffn1_norm"], w["ffn1_wg"], w["ffn1_wu"], w["ffn1_wd"])
    ws, bst = _gmlp_weights(w["gmlp_ws"], w["gmlp_bs"], seq_len)
    outs = _inproj(h, w["mix_norm"], w["w_in"], w["gmlp_v_norm"], ws, bst,
                   emit_zn=cache is not None)
    q, k, v, kb, vb, om = outs[:6]
    if cache is None:
        h, low = _mix_out(h, q, kb, vb, om, w["w_out"])

        def finish_older_keys(h):
            older = _attn_prompt(q, kb, vb, older_only=True)
            return _outproj(h, older, jnp.zeros_like(om), w["w_out"])

        h = lax.cond(jnp.any(low < DEAD_LOG2), finish_older_keys, lambda h: h, h)
        zn = None
    else:
        osb = _attn_sample(q, kb, vb, cache[0], cache[1], rows // seq_len,
                           q.shape[1] // SB_HEAD_DIM)
        zn = outs[6]
        h = _outproj(h, osb, om, w["w_out"])
    h, (w["ffn2_wg"], w["ffn2_wu"], w["ffn2_wd"]) = _ffn(
        h, w["ffn2_norm"], w["ffn2_wg"], w["ffn2_wu"], w["ffn2_wd"])
    h = _ple(h, p, w["ple_norm"], w["ple_w_gate"], w["ple_w_proj"],
             final_gain if final_gain is not None else w["ple_norm"],
             final_norm=final_gain is not None)
    return h, k, v, zn, w


_PRECAST_WEIGHTS = ("w_in", "w_out", "ple_w_gate", "ple_w_proj")


def kernel(x_prompt, x_sample, cache_k, cache_v, p_prompt, p_sample, ffn1_norm, ffn1_wg, ffn1_wu, ffn1_wd, mix_norm, w_in, gmlp_v_norm, gmlp_ws, gmlp_bs, w_out, ffn2_norm, ffn2_wg, ffn2_wu, ffn2_wd, ple_norm, ple_w_gate, ple_w_proj, final_norm):
    weights = dict(ffn1_norm=ffn1_norm, ffn1_wg=ffn1_wg, ffn1_wu=ffn1_wu, ffn1_wd=ffn1_wd,
                   mix_norm=mix_norm, w_in=w_in, gmlp_v_norm=gmlp_v_norm, gmlp_ws=gmlp_ws,
                   gmlp_bs=gmlp_bs, w_out=w_out, ffn2_norm=ffn2_norm, ffn2_wg=ffn2_wg,
                   ffn2_wu=ffn2_wu, ffn2_wd=ffn2_wd, ple_norm=ple_norm,
                   ple_w_gate=ple_w_gate, ple_w_proj=ple_w_proj)
    depth = w_in.shape[0]
    bp, sp, d = x_prompt.shape
    bs, ss, _ = x_sample.shape
    assert bp == 1, "prompt attention handles one sequence"
    heads = cache_k.shape[3]
    width = heads * SB_HEAD_DIM

    hp = x_prompt.reshape(bp * sp, d)
    hs = x_sample.reshape(bs * ss, d)
    kp_l, vp_l, ks_l, vs_l, zs_l = [], [], [], [], []
    for i in range(depth):
        w = {name: (val[i].astype(BF16) if name in _PRECAST_WEIGHTS else val[i])
             for name, val in weights.items()}
        fgain = final_norm if i == depth - 1 else None
        cache = (cache_k[i].reshape(bs, -1, SB_HEAD_DIM), cache_v[i].reshape(bs, -1, SB_HEAD_DIM))
        hs, k_s, v_s, z_s, w = _layer(hs, p_sample[i].reshape(bs * ss, -1), w, seq_len=ss,
                                      final_gain=fgain, cache=cache)
        hp, kp, vp, _, _ = _layer(hp, p_prompt[i].reshape(bp * sp, -1), w, seq_len=sp,
                                  final_gain=fgain)
        kp_l.append(kp.reshape(bp, sp, heads, SB_HEAD_DIM))
        vp_l.append(vp.reshape(bp, sp, heads, SB_HEAD_DIM))
        ks_l.append(k_s.reshape(bs, ss, heads, SB_HEAD_DIM))
        vs_l.append(v_s.reshape(bs, ss, heads, SB_HEAD_DIM))
        zs_l.append(z_s.reshape(bs, ss, -1, MLP_GROUP_DIM))
    return (hp.reshape(bp, sp, d), hs.reshape(bs, ss, d),
            jnp.stack(kp_l), jnp.stack(vp_l), jnp.stack(ks_l), jnp.stack(vs_l), jnp.stack(zs_l))
```

```python
import functools

import jax
import jax.numpy as jnp
from jax import lax
from jax.experimental import pallas as pl
from jax.experimental.pallas import tpu as pltpu

F32 = jnp.float32
BF16 = jnp.bfloat16

EPS = 1e-6
SB_HEAD_DIM = 128
MLP_GROUP_DIM = 128
MLP_CHUNK = 128

V7X_VMEM_BYTES = 64 * 1024 * 1024
VMEM_REQUEST_CAP = V7X_VMEM_BYTES - 6 * 1024 * 1024
LANES = 128

DEAD_LOG2 = 152.0
LOG2_E = 1.4426950408889634

ROW_TILE = 512
FFN_ROW_TILE = 1024
FFN_COL_TILE = 512
FFN_CAST_COL_TILE = 256
ATT_BLOCK = 256
ATT_HEADS_PER_STEP = 4
ATT_QBLOCKS_PER_STEP = 2


def _compiler_params(semantics, vmem_bytes):
    limit = min(int(vmem_bytes * 1.2) + (4 << 20), VMEM_REQUEST_CAP)
    return pltpu.CompilerParams(dimension_semantics=semantics, vmem_limit_bytes=limit)


def _rms_scale(x):
    return lax.rsqrt(jnp.mean(x * x, axis=-1, keepdims=True) + EPS)


def _bf16_weight(w_ref, cast_ref):
    if cast_ref is None:
        return w_ref[...]
    w = w_ref[...].astype(BF16)
    cast_ref[...] = w
    return w


def _ffn_kernel(x_ref, gain_ref, wg_ref, wu_ref, wd_ref, o_ref, *rest, nt, nf, cast_weights):
    if cast_weights:
        wg_cast, wu_cast, wd_cast, n_ref, a0_ref, a1_ref = rest
    else:
        n_ref, a0_ref, a1_ref = rest
        wg_cast = wu_cast = wd_cast = None
    i = pl.program_id(0)
    j = pl.program_id(1)

    def up(a_ref, between=None):
        n = n_ref[...]
        g = jnp.dot(n, _bf16_weight(wg_ref, wg_cast), preferred_element_type=F32)
        if between is not None:
            between()
        u = jnp.dot(n, _bf16_weight(wu_ref, wu_cast), preferred_element_type=F32)
        a_ref[...] = (0.5 * g * jax.nn.sigmoid(g) * u).astype(BF16)

    def down(a_ref):
        o_ref[...] += jnp.dot(a_ref[...], _bf16_weight(wd_ref, wd_cast),
                              preferred_element_type=F32)

    def normalize():
        x = x_ref[...]
        n_ref[...] = (x * _rms_scale(x) * gain_ref[...]).astype(BF16)

    def copy_residual():
        o_ref[...] = x_ref[...]

    @pl.when((j == 0) & (i == 0))
    def _():
        normalize()

    @pl.when(j == 0)
    def _():
        up(a0_ref, between=copy_residual)

    middle = (j > 0) & (j < nf)

    @pl.when(middle & (j % 2 == 1))
    def _():
        down(a0_ref)
        up(a1_ref)

    @pl.when(middle & (j % 2 == 0))
    def _():
        down(a1_ref)
        up(a0_ref)

    last_a = a1_ref if (nf - 1) % 2 else a0_ref

    @pl.when((j == nf) & (i < nt - 1))
    def _():
        down(last_a)
        normalize()

    @pl.when((j == nf) & (i == nt - 1))
    def _():
        down(last_a)


def _ffn(x, gain, wg, wu, wd):
    t, d = x.shape
    f = wg.shape[1]
    cast_weights = wg.dtype != BF16
    tm = min(FFN_ROW_TILE, t)
    tf = FFN_CAST_COL_TILE if cast_weights else FFN_COL_TILE
    assert t % tm == 0 and f % tf == 0
    assert not cast_weights or t == tm
    nf = f // tf
    wbytes = jnp.dtype(wg.dtype).itemsize
    x_bufs = 1 if cast_weights else 2
    vmem = ((x_bufs + 2) * tm * d * 4 + tm * d * 2 + 2 * 3 * d * tf * wbytes
            + 2 * tm * tf * 2 + 2 * tm * tf * 4)
    up_blk = lambda i, j: (0, jnp.minimum(j, nf - 1))
    down_blk = lambda i, j: (jnp.maximum(j - 1, 0), 0)
    row_blk = lambda i, j: (i, 0)
    out_specs = [pl.BlockSpec((tm, d), row_blk)]
    out_shape = [jax.ShapeDtypeStruct((t, d), F32)]
    if cast_weights:
        vmem += 2 * 3 * d * tf * 2
        out_specs += [pl.BlockSpec((d, tf), up_blk), pl.BlockSpec((d, tf), up_blk),
                      pl.BlockSpec((tf, d), down_blk)]
        out_shape += [jax.ShapeDtypeStruct(w.shape, BF16) for w in (wg, wu, wd)]
    nt = t // tm
    x_blk = lambda i, j: (jnp.minimum(jnp.where(j == nf, i + 1, i), nt - 1), 0)
    x_spec = (pl.BlockSpec((tm, d), x_blk, pipeline_mode=pl.Buffered(1)) if cast_weights
              else pl.BlockSpec((tm, d), x_blk))
    outs = pl.pallas_call(
        functools.partial(_ffn_kernel, nt=nt, nf=nf, cast_weights=cast_weights),
        grid=(nt, nf + 1),
        in_specs=[
            x_spec,
            pl.BlockSpec((1, d), lambda i, j: (0, 0)),
            pl.BlockSpec((d, tf), up_blk),
            pl.BlockSpec((d, tf), up_blk),
            pl.BlockSpec((tf, d), down_blk),
        ],
        out_specs=out_specs,
        out_shape=out_shape,
        scratch_shapes=[pltpu.VMEM((tm, d), BF16), pltpu.VMEM((tm, tf), BF16),
                        pltpu.VMEM((tm, tf), BF16)],
        compiler_params=_compiler_params(("arbitrary", "arbitrary"), vmem),
        name="ffn",
    )(x, gain.reshape(1, d), wg, wu, wd)
    return (outs[0], tuple(outs[1:])) if cast_weights else (outs[0], (wg, wu, wd))


INPROJ_GROUPS = dict(q=0, k=1, v=2, u=3, z=4)


def _gmlp_mix(u, zn, ws_ref, bst_ref, om_ref):
    znb = zn.astype(BF16)
    row = lax.broadcasted_iota(jnp.int32, (MLP_CHUNK, MLP_CHUNK), 0)
    col = lax.broadcasted_iota(jnp.int32, (MLP_CHUNK, MLP_CHUNK), 1)
    lower = col <= row
    for g in range(ws_ref.shape[0]):
        w = jnp.where(lower, ws_ref[g], 0.0).astype(BF16)
        bias = bst_ref[:, g:g + 1]
        cs = slice(g * MLP_GROUP_DIM, (g + 1) * MLP_GROUP_DIM)
        for c in range(znb.shape[0] // MLP_CHUNK):
            rs = slice(c * MLP_CHUNK, (c + 1) * MLP_CHUNK)
            mixed = jnp.dot(w, znb[rs, cs], preferred_element_type=F32) + bias
            om_ref[rs, cs] = (u[rs, cs] * mixed).astype(BF16)


def _inproj_kernel(h_ref, gain_ref, w_ref, vgain_ref, ws_ref, bst_ref,
                   q_ref, k_ref, v_ref, kb_ref, vb_ref, om_ref, *zn_out):
    width = q_ref.shape[1]
    h = h_ref[...]
    n = (h * _rms_scale(h) * gain_ref[...]).astype(BF16)

    def project(name):
        g = INPROJ_GROUPS[name]
        return jnp.dot(n, w_ref[:, g * width:(g + 1) * width], preferred_element_type=F32)

    def store_state(ref, y):
        heads = width // SB_HEAD_DIM
        for hd in range(heads):
            ref[pl.ds(hd, y.shape[0], stride=heads), :] = y[:, hd * SB_HEAD_DIM:(hd + 1) * SB_HEAD_DIM]

    u = jax.nn.gelu(project("u"), approximate=True)
    gz = jax.nn.gelu(project("z"), approximate=True)
    zn = gz * _rms_scale(gz) * vgain_ref[...]
    if zn_out:
        store_state(zn_out[0], zn)
    _gmlp_mix(u, zn, ws_ref, bst_ref, om_ref)
    q_ref[...] = (project("q") * (SB_HEAD_DIM ** -0.5 * LOG2_E)).astype(BF16)
    k = project("k")
    store_state(k_ref, k)
    kb_ref[...] = k.astype(BF16)
    v = project("v")
    store_state(v_ref, v)
    vb_ref[...] = v.astype(BF16)


def _inproj(h, gain, w_in, vgain, ws, bst, *, emit_zn):
    t, d = h.shape
    width = w_in.shape[1] // len(INPROJ_GROUPS)
    tm = min(ROW_TILE, t)
    assert t % tm == 0 and tm % MLP_CHUNK == 0 and w_in.shape[1] == len(INPROJ_GROUPS) * width
    groups = ws.shape[0]
    row_blk = lambda i: (i, 0)
    const2 = lambda i: (0, 0)
    per_row = width // SB_HEAD_DIM
    state = jax.ShapeDtypeStruct((t * per_row, SB_HEAD_DIM), F32)
    out_shape = [
        jax.ShapeDtypeStruct((t, width), BF16),
        state,
        state,
        jax.ShapeDtypeStruct((t, width), BF16),
        jax.ShapeDtypeStruct((t, width), BF16),
        jax.ShapeDtypeStruct((t, width), BF16),
    ]
    if emit_zn:
        out_shape.append(state)
    out_specs = [pl.BlockSpec((tm * per_row, SB_HEAD_DIM), row_blk) if s is state
                 else pl.BlockSpec((tm, width), row_blk) for s in out_shape]
    out_bytes = sum(tm * width * jnp.dtype(s.dtype).itemsize for s in out_shape)
    vmem = (2 * tm * d * 4 + w_in.size * 2 + 2 * out_bytes + tm * d * 2 + 6 * tm * width * 4)
    return pl.pallas_call(
        _inproj_kernel,
        grid=(t // tm,),
        in_specs=[
            pl.BlockSpec((tm, d), row_blk),
            pl.BlockSpec((1, d), const2),
            pl.BlockSpec(w_in.shape, const2, pipeline_mode=pl.Buffered(1)),
            pl.BlockSpec((1, width), const2),
            pl.BlockSpec((groups, MLP_CHUNK, MLP_CHUNK), lambda i: (0, 0, 0)),
            pl.BlockSpec((MLP_CHUNK, groups), const2),
        ],
        out_specs=out_specs,
        out_shape=out_shape,
        compiler_params=_compiler_params(("parallel",), vmem),
        name="inproj",
    )(h, gain.reshape(1, d), w_in, vgain.reshape(1, width), ws, bst)


def _init_tri(tri_ref):
    j = lax.broadcasted_iota(jnp.int32, tri_ref.shape, 0)
    s = lax.broadcasted_iota(jnp.int32, tri_ref.shape, 1)
    tri_ref[...] = jnp.where(j > s, -1.0, 0.0).astype(BF16)


def _sb_span(q, k, v, valids, tri, carry):
    tb = tri.shape[1]
    u = lax.dot_general(q, k, (((1,), (1,)), ((), ())), preferred_element_type=F32)
    s = jnp.maximum(u, 0.0) + jnp.log2(1.0 + jnp.exp2(jnp.minimum(u, -u)))
    weights = [None] * len(valids)
    for b in reversed(range(len(valids))):
        cols = slice(b * tb, (b + 1) * tb)
        sb = s[:, cols]
        if valids[b] is not None:
            sb = jnp.where(valids[b], sb, 0.0)
        newer = jnp.dot(sb.astype(BF16), tri, preferred_element_type=F32)
        log2_a = u[:, cols] - sb + newer
        total = jnp.sum(sb, axis=1, keepdims=True)
        if carry is None:
            carry = total
        else:
            log2_a = log2_a - carry
            carry = carry + total
        a = jnp.exp2(log2_a)
        if valids[b] is not None:
            a = jnp.where(valids[b], a, 0.0)
        weights[b] = a.astype(BF16)
    a = weights[0] if len(weights) == 1 else jnp.concatenate(weights, axis=1)
    return jnp.dot(a, v, preferred_element_type=F32), carry


def _head_rows(ref, h, rows):
    return ref.at[pl.ds(h * rows, rows)]


def _sb_older_blocks(q, load_kv, first_kb, tri, carry_ref, acc_ref):
    def cond(state):
        kb, low = state
        return (kb >= 0) & (low < DEAD_LOG2)

    def body(state):
        kb, _ = state
        k, v = load_kv(kb)
        o, carry = _sb_span(q, k, v, [None], tri, carry_ref[...])
        acc_ref[...] += o
        carry_ref[...] = carry
        return kb - 1, jnp.min(carry)

    return lax.while_loop(cond, body, (first_kb, jnp.min(carry_ref[...])))[1]


def _attn_prompt_kernel(q_ref, k_ref, v_ref, o_ref, tri_ref, carry_ref, acc_ref, *, older_only):
    i = pl.program_id(1)
    tb = tri_ref.shape[0]
    nqb = q_ref.shape[0] // tb
    heads = q_ref.shape[1] // SB_HEAD_DIM

    @pl.when((pl.program_id(0) == 0) & (i == 0))
    def _():
        _init_tri(tri_ref)

    tri = tri_ref[...]
    causal = (lax.broadcasted_iota(jnp.int32, (tb, tb), 1)
              < lax.broadcasted_iota(jnp.int32, (tb, tb), 0))
    chains = [(b, h) for b in range(nqb) for h in range(heads)]

    def q_rows(b):
        return slice(b * tb, (b + 1) * tb)

    def head_cols(h):
        return slice(h * SB_HEAD_DIM, (h + 1) * SB_HEAD_DIM)

    def first_span(b, h, with_previous_block):
        hc = head_cols(h)
        qblock = i * nqb + b
        if with_previous_block:
            keys = pl.ds(pl.multiple_of((qblock - 1) * tb, tb), 2 * tb)
            valids = [None, causal]
        else:
            keys = pl.ds(pl.multiple_of(qblock * tb, tb), tb)
            valids = [causal]
        o, carry = _sb_span(q_ref[q_rows(b), hc], k_ref[keys, hc], v_ref[keys, hc], valids,
                            tri, None)
        slot = b * heads + h
        if older_only:
            o = jnp.zeros_like(o)
        o_ref[q_rows(b), hc] = o.astype(o_ref.dtype)
        _head_rows(acc_ref, slot, tb)[...] = o
        _head_rows(carry_ref, slot, tb)[...] = carry
        return carry

    def run(first_block_has_previous):
        lowest = None
        for b, h in chains:
            carry = first_span(b, h, first_block_has_previous or b > 0)
            if first_block_has_previous or b >= 2:
                lowest = carry if lowest is None else jnp.minimum(lowest, carry)
        if lowest is None:
            return

        @pl.when(jnp.min(lowest) < DEAD_LOG2)
        def _():
            for b, h in chains:
                hc = head_cols(h)
                slot = b * heads + h

                def load_kv(kb, hc=hc):
                    rows = pl.ds(pl.multiple_of(kb * tb, tb), tb)
                    return k_ref[rows, hc], v_ref[rows, hc]

                acc = _head_rows(acc_ref, slot, tb)
                _sb_older_blocks(q_ref[q_rows(b), hc], load_kv, i * nqb + b - 2, tri,
                                 _head_rows(carry_ref, slot, tb), acc)
                o_ref[q_rows(b), hc] = acc[...].astype(o_ref.dtype)

    @pl.when(i == 0)
    def _():
        run(False)

    @pl.when(i > 0)
    def _():
        run(True)


def _attn_prompt(q, k, v, *, older_only=False):
    s, width = q.shape
    hb = ATT_HEADS_PER_STEP
    tb = ATT_BLOCK
    tq = ATT_QBLOCKS_PER_STEP * tb
    assert s % tq == 0 and width % (hb * SB_HEAD_DIM) == 0
    wb = hb * SB_HEAD_DIM
    chains = ATT_QBLOCKS_PER_STEP * hb
    resident = pl.Buffered(1)
    vmem = 2 * s * wb * 2 + 2 * 2 * tq * wb * 2 + tb * tb * 2 + chains * 8 * tb * 2 * tb * 4
    return pl.pallas_call(
        functools.partial(_attn_prompt_kernel, older_only=older_only),
        grid=(width // wb, s // tq),
        in_specs=[
            pl.BlockSpec((tq, wb), lambda g, i: (i, g)),
            pl.BlockSpec((s, wb), lambda g, i: (0, g), pipeline_mode=resident),
            pl.BlockSpec((s, wb), lambda g, i: (0, g), pipeline_mode=resident),
        ],
        out_specs=pl.BlockSpec((tq, wb), lambda g, i: (i, g)),
        out_shape=jax.ShapeDtypeStruct((s, width), BF16),
        scratch_shapes=[
            pltpu.VMEM((tb, tb), BF16),
            pltpu.VMEM((chains * tb, 1), F32),
            pltpu.VMEM((chains * tb, SB_HEAD_DIM), F32),
        ],
        compiler_params=_compiler_params(("arbitrary", "arbitrary"), vmem),
        name="attn_prompt",
    )(q, k, v)


def _attn_sample_kernel(q_ref, kn_ref, vn_ref, ck_ref, cv_ref, o_ref, *rest, heads, older_only):
    if older_only:
        tri_ref, carry_ref, acc_ref = rest
    else:
        low_ref, tri_ref = rest
    tq = q_ref.shape[0]
    held = ck_ref.shape[1] // heads
    tb = tri_ref.shape[1]
    head_rows = tb - tq
    first = held - head_rows

    @pl.when(pl.program_id(0) == 0)
    def _():
        _init_tri(tri_ref)

    tri = tri_ref[...]
    row = lax.broadcasted_iota(jnp.int32, (tq, tb), 0)
    col = lax.broadcasted_iota(jnp.int32, (tq, tb), 1)

    def cache_rows(ref, h, start, size):
        return ref[0, pl.ds(start * heads + h, size, stride=heads), :].astype(BF16)

    lowest = None
    for h in range(heads):
        hc = slice(h * SB_HEAD_DIM, (h + 1) * SB_HEAD_DIM)
        k0 = jnp.concatenate([cache_rows(ck_ref, h, first, head_rows), kn_ref[:, hc]], axis=0)
        v0 = jnp.concatenate([cache_rows(cv_ref, h, first, head_rows), vn_ref[:, hc]], axis=0)
        o, carry = _sb_span(q_ref[:, hc], k0, v0, [col < row + head_rows], tri, None)
        if older_only:
            o = jnp.zeros_like(o)
            _head_rows(acc_ref, h, tq)[...] = o
            _head_rows(carry_ref, h, tq)[...] = carry
        o_ref[:, hc] = o.astype(o_ref.dtype)
        lowest = carry if lowest is None else jnp.minimum(lowest, carry)

    if not older_only:
        low_ref[...] = jnp.broadcast_to(jnp.min(lowest, axis=0, keepdims=True), low_ref.shape)
        return

    nfull = first // tb
    rem = first - nfull * tb

    @pl.when(jnp.min(lowest) < DEAD_LOG2)
    def _():
        for h in range(heads):
            hc = slice(h * SB_HEAD_DIM, (h + 1) * SB_HEAD_DIM)
            q = q_ref[:, hc]
            acc = _head_rows(acc_ref, h, tq)
            carry_h = _head_rows(carry_ref, h, tq)

            def load_kv(kb, h=h):
                start = rem + kb * tb
                return cache_rows(ck_ref, h, start, tb), cache_rows(cv_ref, h, start, tb)

            low = _sb_older_blocks(q, load_kv, nfull - 1, tri, carry_h, acc)

            if rem:
                @pl.when(low < DEAD_LOG2)
                def _():
                    o, _ = _sb_span(q, cache_rows(ck_ref, h, 0, tb), cache_rows(cv_ref, h, 0, tb),
                                    [col < rem], tri, carry_h[...])
                    acc[...] += o

            o_ref[:, hc] = acc[...].astype(o_ref.dtype)


def _attn_sample(q, kn, vn, cache_k, cache_v, batch, heads):
    rows, width = q.shape
    tq = rows // batch
    past = cache_k.shape[1] // heads
    tb = ATT_BLOCK
    assert tq < tb and past >= 2 * tb and past % tb == 0 and tq % 16 == 0
    assert width == heads * SB_HEAD_DIM
    new_blk = lambda b: (b, 0)
    new_specs = [pl.BlockSpec((tq, width), new_blk)] * 3

    def call(older_only):
        held = past if older_only else tb
        cache_spec = pl.BlockSpec((1, held * heads, SB_HEAD_DIM),
                                  lambda b: (b, past // held - 1, 0))
        out_specs = [pl.BlockSpec((tq, width), new_blk)]
        out_shape = [jax.ShapeDtypeStruct((rows, width), BF16)]
        scratch = [pltpu.VMEM((tb, tb), BF16)]
        if older_only:
            scratch += [pltpu.VMEM((heads * tq, 1), F32),
                        pltpu.VMEM((heads * tq, SB_HEAD_DIM), F32)]
        else:
            out_specs.append(pl.BlockSpec((8, LANES), new_blk))
            out_shape.append(jax.ShapeDtypeStruct((batch * 8, LANES), F32))
        vmem = 2 * 2 * held * width * 4 + tb * tb * 2 + heads * 14 * tq * tb * 4
        return pl.pallas_call(
            functools.partial(_attn_sample_kernel, heads=heads, older_only=older_only),
            grid=(batch,),
            in_specs=new_specs + [cache_spec, cache_spec],
            out_specs=out_specs,
            out_shape=out_shape,
            scratch_shapes=scratch,
            compiler_params=_compiler_params(("arbitrary",), vmem),
            name="attn_sample_older" if older_only else "attn_sample",
        )(q, kn, vn, cache_k, cache_v)

    o, low = call(False)

    def add_older_rows(o):
        older = call(True)[0]
        return (o.astype(F32) + older.astype(F32)).astype(BF16)

    return lax.cond(jnp.any(low < DEAD_LOG2), add_older_rows, lambda o: o, o)


def _outproj_kernel(h_ref, osb_ref, om_ref, wa_ref, wb_ref, o_ref):
    o_ref[...] = (h_ref[...]
                  + jnp.dot(osb_ref[...], wa_ref[...], preferred_element_type=F32)
                  + jnp.dot(om_ref[...], wb_ref[...], preferred_element_type=F32))


def _outproj(h, osb, om, w_out):
    t, d = h.shape
    width = osb.shape[1]
    tm = min(ROW_TILE, t)
    assert t % tm == 0 and w_out.shape == (2 * width, d)
    vmem = 2 * 2 * tm * d * 4 + 2 * 2 * tm * width * 2 + 2 * 2 * width * d * 2
    return pl.pallas_call(
        _outproj_kernel,
        grid=(t // tm,),
        in_specs=[
            pl.BlockSpec((tm, d), lambda i: (i, 0)),
            pl.BlockSpec((tm, width), lambda i: (i, 0)),
            pl.BlockSpec((tm, width), lambda i: (i, 0)),
            pl.BlockSpec((width, d), lambda i: (0, 0)),
            pl.BlockSpec((width, d), lambda i: (1, 0)),
        ],
        out_specs=pl.BlockSpec((tm, d), lambda i: (i, 0)),
        out_shape=jax.ShapeDtypeStruct((t, d), F32),
        compiler_params=_compiler_params(("parallel",), vmem),
        name="outproj",
    )(h, osb, om, w_out, w_out)


def _mix_out_kernel(q_ref, kc_ref, vc_ref, kp_ref, vp_ref, h_ref, om_ref, w_ref,
                    o_ref, low_ref, tri_ref, osb0_ref, osb1_ref, *, nt):
    s = pl.program_id(0)
    tb = tri_ref.shape[0]
    width = q_ref.shape[1]
    heads = width // SB_HEAD_DIM
    nqb = q_ref.shape[0] // tb

    @pl.when(s == 0)
    def _():
        _init_tri(tri_ref)

    def project_columns(osb_ref, c, nchunks):
        cols = slice(c * (o_ref.shape[1] // nchunks), (c + 1) * (o_ref.shape[1] // nchunks))
        o_ref[:, cols] = (h_ref[:, cols]
                          + jnp.dot(osb_ref[...], w_ref[0:width, cols], preferred_element_type=F32)
                          + jnp.dot(om_ref[...], w_ref[width:, cols], preferred_element_type=F32))

    def attend(osb_ref, first_tile, project_from=None):
        tri = tri_ref[...]
        causal = (lax.broadcasted_iota(jnp.int32, (tb, tb), 1)
                  < lax.broadcasted_iota(jnp.int32, (tb, tb), 0))
        lowest = None
        nchunks = nqb * heads // 2
        for b in range(nqb):
            rows = slice(b * tb, (b + 1) * tb)
            for h in range(heads):
                if project_from is not None and (b * heads + h) % 2 == 0:
                    project_columns(project_from, (b * heads + h) // 2, nchunks)
                hc = slice(h * SB_HEAD_DIM, (h + 1) * SB_HEAD_DIM)
                if b > 0:
                    keys = slice((b - 1) * tb, (b + 1) * tb)
                    k, v, valids = kc_ref[keys, hc], vc_ref[keys, hc], [None, causal]
                elif first_tile:
                    k, v, valids = kc_ref[0:tb, hc], vc_ref[0:tb, hc], [causal]
                else:
                    k = jnp.concatenate([kp_ref[:, hc], kc_ref[0:tb, hc]], axis=0)
                    v = jnp.concatenate([vp_ref[:, hc], vc_ref[0:tb, hc]], axis=0)
                    valids = [None, causal]
                o, carry = _sb_span(q_ref[rows, hc], k, v, valids, tri, None)
                osb_ref[rows, hc] = o.astype(BF16)
                if not (first_tile and b < 2):
                    lowest = carry if lowest is None else jnp.minimum(lowest, carry)
        if lowest is None:
            low_ref[...] = jnp.full(low_ref.shape, DEAD_LOG2, F32)
        else:
            low_ref[...] = jnp.broadcast_to(jnp.min(lowest, axis=0, keepdims=True),
                                            low_ref.shape)

    def project(osb_ref):
        o_ref[...] = (h_ref[...]
                      + jnp.dot(osb_ref[...], w_ref[0:width], preferred_element_type=F32)
                      + jnp.dot(om_ref[...], w_ref[width:], preferred_element_type=F32))

    @pl.when(s == 0)
    def _():
        attend(osb0_ref, True)

    middle = (s > 0) & (s < nt)

    @pl.when(middle & (s % 2 == 1))
    def _():
        attend(osb1_ref, False, project_from=osb0_ref)

    @pl.when(middle & (s % 2 == 0))
    def _():
        attend(osb0_ref, False, project_from=osb1_ref)

    @pl.when(s == nt)
    def _():
        project(osb1_ref if (nt - 1) % 2 else osb0_ref)


def _mix_out(h, q, kb, vb, om, w_out):
    t, d = h.shape
    width = q.shape[1]
    tb = ATT_BLOCK
    tm = ATT_QBLOCKS_PER_STEP * tb
    assert t % tm == 0 and w_out.shape == (2 * width, d)
    nt = t // tm
    sub = tm // tb
    cur = lambda s: (jnp.minimum(s, nt - 1), 0)
    prev_blk = lambda s: (jnp.maximum(jnp.minimum(s, nt - 1) * sub - 1, 0), 0)
    done = lambda s: (jnp.maximum(s - 1, 0), 0)
    vmem = (2 * (3 * tm + 2 * tb) * width * 2 + 2 * 2 * tm * d * 4 + 2 * tm * width * 2
            + w_out.size * 2 + 2 * tm * width * 2 + 16 * 8 * tb * 2 * tb * 4 // 4)
    return pl.pallas_call(
        functools.partial(_mix_out_kernel, nt=nt),
        grid=(nt + 1,),
        in_specs=[
            pl.BlockSpec((tm, width), cur),
            pl.BlockSpec((tm, width), cur),
            pl.BlockSpec((tm, width), cur),
            pl.BlockSpec((tb, width), prev_blk),
            pl.BlockSpec((tb, width), prev_blk),
            pl.BlockSpec((tm, d), done),
            pl.BlockSpec((tm, width), done),
            pl.BlockSpec(w_out.shape, lambda s: (0, 0), pipeline_mode=pl.Buffered(1)),
        ],
        out_specs=[pl.BlockSpec((tm, d), done), pl.BlockSpec((8, LANES), cur)],
        out_shape=[jax.ShapeDtypeStruct((t, d), F32), jax.ShapeDtypeStruct((nt * 8, LANES), F32)],
        scratch_shapes=[pltpu.VMEM((tb, tb), BF16), pltpu.VMEM((tm, width), BF16),
                        pltpu.VMEM((tm, width), BF16)],
        compiler_params=_compiler_params(("arbitrary",), vmem),
        name="mix_out",
    )(q, kb, vb, kb, vb, h, om, w_out)


def _ple_kernel(h_ref, p_ref, gain_ref, wg_ref, wp_ref, fgain_ref, o_ref, *, final_norm):
    h = h_ref[...]
    n = (h * _rms_scale(h) * gain_ref[...]).astype(BF16)
    gate = jax.nn.sigmoid(jnp.dot(n, wg_ref[...], preferred_element_type=F32))
    proj = jnp.dot(p_ref[...].astype(BF16), wp_ref[...], preferred_element_type=F32)
    o = h + gate * proj
    if final_norm:
        o = o * _rms_scale(o) * fgain_ref[...]
    o_ref[...] = o


def _ple(h, p, gain, w_gate, w_proj, fgain, *, final_norm):
    t, d = h.shape
    pd = p.shape[1]
    tm = min(ROW_TILE, t)
    assert t % tm == 0
    vmem = 2 * 2 * tm * d * 4 + 2 * tm * pd * 4 + 2 * d * d * 2 + 2 * pd * d * 2 + 3 * tm * d * 4
    return pl.pallas_call(
        functools.partial(_ple_kernel, final_norm=final_norm),
        grid=(t // tm,),
        in_specs=[
            pl.BlockSpec((tm, d), lambda i: (i, 0)),
            pl.BlockSpec((tm, pd), lambda i: (i, 0)),
            pl.BlockSpec((1, d), lambda i: (0, 0)),
            pl.BlockSpec((d, d), lambda i: (0, 0)),
            pl.BlockSpec((pd, d), lambda i: (0, 0)),
            pl.BlockSpec((1, d), lambda i: (0, 0)),
        ],
        out_specs=pl.BlockSpec((tm, d), lambda i: (i, 0)),
        out_shape=jax.ShapeDtypeStruct((t, d), F32),
        compiler_params=_compiler_params(("parallel",), vmem),
        name="ple",
    )(h, p, gain.reshape(1, d), w_gate, w_proj, fgain.reshape(1, d))


def _gmlp_weights(ws, bs, seq_len):
    length = min(seq_len, MLP_CHUNK)
    assert MLP_CHUNK % length == 0
    reps = MLP_CHUNK // length
    if reps > 1:
        eye = jnp.eye(reps, dtype=ws.dtype)
        ws = jnp.einsum("ab,gij->gaibj", eye, ws[:, :length, :length]).reshape(
            ws.shape[0], MLP_CHUNK, MLP_CHUNK)
        bs = jnp.tile(bs[:, :length], (1, reps))
    return ws, jnp.transpose(bs)


def _layer(x, p, w, *, seq_len, final_gain, cache=None):
    rows = x.shape[0]
    w = dict(w)
    h, (w["ffn1_wg"], w["ffn1_wu"], w["ffn1_wd"]) = _ffn(
        x, w["ffn1_norm"], w["ffn1_wg"], w["ffn1_wu"], w["ffn1_wd"])
    ws, bst = _gmlp_weights(w["gmlp_ws"], w["gmlp_bs"], seq_len)
    outs = _inproj(h, w["mix_norm"], w["w_in"], w["gmlp_v_norm"], ws, bst,
                   emit_zn=cache is not None)
    q, k, v, kb, vb, om = outs[:6]
    if cache is None:
        h, low = _mix_out(h, q, kb, vb, om, w["w_out"])

        def finish_older_keys(h):
            older = _attn_prompt(q, kb, vb, older_only=True)
            return _outproj(h, older, jnp.zeros_like(om), w["w_out"])

        h = lax.cond(jnp.any(low < DEAD_LOG2), finish_older_keys, lambda h: h, h)
        zn = None
    else:
        osb = _attn_sample(q, kb, vb, cache[0], cache[1], rows // seq_len,
                           q.shape[1] // SB_HEAD_DIM)
        zn = outs[6]
        h = _outproj(h, osb, om, w["w_out"])
    h, (w["ffn2_wg"], w["ffn2_wu"], w["ffn2_wd"]) = _ffn(
        h, w["ffn2_norm"], w["ffn2_wg"], w["ffn2_wu"], w["ffn2_wd"])
    h = _ple(h, p, w["ple_norm"], w["ple_w_gate"], w["ple_w_proj"],
             final_gain if final_gain is not None else w["ple_norm"],
             final_norm=final_gain is not None)
    return h, k, v, zn, w


_PRECAST_WEIGHTS = ("w_in", "w_out", "ple_w_gate", "ple_w_proj")


def kernel(x_prompt, x_sample, cache_k, cache_v, p_prompt, p_sample, ffn1_norm, ffn1_wg, ffn1_wu, ffn1_wd, mix_norm, w_in, gmlp_v_norm, gmlp_ws, gmlp_bs, w_out, ffn2_norm, ffn2_wg, ffn2_wu, ffn2_wd, ple_norm, ple_w_gate, ple_w_proj, final_norm):
    weights = dict(ffn1_norm=ffn1_norm, ffn1_wg=ffn1_wg, ffn1_wu=ffn1_wu, ffn1_wd=ffn1_wd,
                   mix_norm=mix_norm, w_in=w_in, gmlp_v_norm=gmlp_v_norm, gmlp_ws=gmlp_ws,
                   gmlp_bs=gmlp_bs, w_out=w_out, ffn2_norm=ffn2_norm, ffn2_wg=ffn2_wg,
                   ffn2_wu=ffn2_wu, ffn2_wd=ffn2_wd, ple_norm=ple_norm,
                   ple_w_gate=ple_w_gate, ple_w_proj=ple_w_proj)
    depth = w_in.shape[0]
    bp, sp, d = x_prompt.shape
    bs, ss, _ = x_sample.shape
    assert bp == 1, "prompt attention handles one sequence"
    heads = cache_k.shape[3]
    width = heads * SB_HEAD_DIM

    hp = x_prompt.reshape(bp * sp, d)
    hs = x_sample.reshape(bs * ss, d)
    kp_l, vp_l, ks_l, vs_l, zs_l = [], [], [], [], []
    for i in range(depth):
        w = {name: (val[i].astype(BF16) if name in _PRECAST_WEIGHTS else val[i])
             for name, val in weights.items()}
        fgain = final_norm if i == depth - 1 else None
        cache = (cache_k[i].reshape(bs, -1, SB_HEAD_DIM), cache_v[i].reshape(bs, -1, SB_HEAD_DIM))
        hs, k_s, v_s, z_s, w = _layer(hs, p_sample[i].reshape(bs * ss, -1), w, seq_len=ss,
                                      final_gain=fgain, cache=cache)
        hp, kp, vp, _, _ = _layer(hp, p_prompt[i].reshape(bp * sp, -1), w, seq_len=sp,
                                  final_gain=fgain)
        kp_l.append(kp.reshape(bp, sp, heads, SB_HEAD_DIM))
        vp_l.append(vp.reshape(bp, sp, heads, SB_HEAD_DIM))
        ks_l.append(k_s.reshape(bs, ss, heads, SB_HEAD_DIM))
        vs_l.append(v_s.reshape(bs, ss, heads, SB_HEAD_DIM))
        zs_l.append(z_s.reshape(bs, ss, -1, MLP_GROUP_DIM))
    return (hp.reshape(bp, sp, d), hs.reshape(bs, ss, d),
            jnp.stack(kp_l), jnp.stack(vp_l), jnp.stack(ks_l), jnp.stack(vs_l), jnp.stack(zs_l))
```

```python
import functools

import jax
import jax.numpy as jnp
from jax import lax
from jax.experimental import pallas as pl
from jax.experimental.pallas import tpu as pltpu

F32 = jnp.float32
BF16 = jnp.bfloat16

EPS = 1e-6
SB_HEAD_DIM = 128
MLP_GROUP_DIM = 128
MLP_CHUNK = 128

V7X_VMEM_BYTES = 64 * 1024 * 1024
VMEM_REQUEST_CAP = V7X_VMEM_BYTES - 6 * 1024 * 1024
LANES = 128
SUBLANES = 8

DEAD_LOG2 = 152.0
LOG2_E = 1.4426950408889634

ROW_TILE = 512
FFN_ROW_TILE = 1024
FFN_COL_TILE = 512
FFN_CAST_COL_TILE = 256
ATT_BLOCK = 256
ATT_HEADS_PER_STEP = 4
ATT_QBLOCKS_PER_STEP = 2


def _compiler_params(semantics, vmem_bytes):
    limit = min(int(vmem_bytes * 1.2) + (4 << 20), VMEM_REQUEST_CAP)
    return pltpu.CompilerParams(dimension_semantics=semantics, vmem_limit_bytes=limit)


def _rms_scale(x):
    return lax.rsqrt(jnp.mean(x * x, axis=-1, keepdims=True) + EPS)


def _bf16_weight(w_ref, cast_ref):
    if cast_ref is None:
        return w_ref[...]
    w = w_ref[...].astype(BF16)
    cast_ref[...] = w
    return w


def _ffn_kernel(x_ref, gain_ref, wg_ref, wu_ref, wd_ref, o_ref, *rest, nt, nf, cast_weights):
    if cast_weights:
        wg_cast, wu_cast, wd_cast, n_ref, a0_ref, a1_ref = rest
    else:
        n_ref, a0_ref, a1_ref = rest
        wg_cast = wu_cast = wd_cast = None
    i = pl.program_id(0)
    j = pl.program_id(1)

    def up(a_ref, between=None):
        n = n_ref[...]
        g = jnp.dot(n, _bf16_weight(wg_ref, wg_cast), preferred_element_type=F32)
        if between is not None:
            between()
        u = jnp.dot(n, _bf16_weight(wu_ref, wu_cast), preferred_element_type=F32)
        a_ref[...] = (0.5 * g * jax.nn.sigmoid(g) * u).astype(BF16)

    def down(a_ref):
        o_ref[...] += jnp.dot(a_ref[...], _bf16_weight(wd_ref, wd_cast),
                              preferred_element_type=F32)

    def normalize():
        x = x_ref[...]
        n_ref[...] = (x * _rms_scale(x) * gain_ref[...]).astype(BF16)

    def copy_residual():
        o_ref[...] = x_ref[...]

    @pl.when((j == 0) & (i == 0))
    def _():
        normalize()

    @pl.when(j == 0)
    def _():
        up(a0_ref, between=copy_residual)

    middle = (j > 0) & (j < nf)

    @pl.when(middle & (j % 2 == 1))
    def _():
        down(a0_ref)
        up(a1_ref)

    @pl.when(middle & (j % 2 == 0))
    def _():
        down(a1_ref)
        up(a0_ref)

    last_a = a1_ref if (nf - 1) % 2 else a0_ref

    @pl.when((j == nf) & (i < nt - 1))
    def _():
        down(last_a)
        normalize()

    @pl.when((j == nf) & (i == nt - 1))
    def _():
        down(last_a)


def _ffn(x, gain, wg, wu, wd):
    t, d = x.shape
    f = wg.shape[1]
    cast_weights = wg.dtype != BF16
    tm = min(FFN_ROW_TILE, t)
    tf = FFN_CAST_COL_TILE if cast_weights else FFN_COL_TILE
    assert t % tm == 0 and f % tf == 0
    assert not cast_weights or t == tm
    nf = f // tf
    wbytes = jnp.dtype(wg.dtype).itemsize
    x_bufs = 1 if cast_weights else 2
    vmem = ((x_bufs + 2) * tm * d * 4 + tm * d * 2 + 2 * 3 * d * tf * wbytes
            + 2 * tm * tf * 2 + 2 * tm * tf * 4)
    up_blk = lambda i, j: (0, jnp.minimum(j, nf - 1))
    down_blk = lambda i, j: (jnp.maximum(j - 1, 0), 0)
    row_blk = lambda i, j: (i, 0)
    out_specs = [pl.BlockSpec((tm, d), row_blk)]
    out_shape = [jax.ShapeDtypeStruct((t, d), F32)]
    if cast_weights:
        vmem += 2 * 3 * d * tf * 2
        out_specs += [pl.BlockSpec((d, tf), up_blk), pl.BlockSpec((d, tf), up_blk),
                      pl.BlockSpec((tf, d), down_blk)]
        out_shape += [jax.ShapeDtypeStruct(w.shape, BF16) for w in (wg, wu, wd)]
    nt = t // tm
    x_blk = lambda i, j: (jnp.minimum(jnp.where(j == nf, i + 1, i), nt - 1), 0)
    x_spec = (pl.BlockSpec((tm, d), x_blk, pipeline_mode=pl.Buffered(1)) if cast_weights
              else pl.BlockSpec((tm, d), x_blk))
    outs = pl.pallas_call(
        functools.partial(_ffn_kernel, nt=nt, nf=nf, cast_weights=cast_weights),
        grid=(nt, nf + 1),
        in_specs=[
            x_spec,
            pl.BlockSpec((1, d), lambda i, j: (0, 0)),
            pl.BlockSpec((d, tf), up_blk),
            pl.BlockSpec((d, tf), up_blk),
            pl.BlockSpec((tf, d), down_blk),
        ],
        out_specs=out_specs,
        out_shape=out_shape,
        scratch_shapes=[pltpu.VMEM((tm, d), BF16), pltpu.VMEM((tm, tf), BF16),
                        pltpu.VMEM((tm, tf), BF16)],
        compiler_params=_compiler_params(("arbitrary", "arbitrary"), vmem),
        name="ffn",
    )(x, gain.reshape(1, d), wg, wu, wd)
    return (outs[0], tuple(outs[1:])) if cast_weights else (outs[0], (wg, wu, wd))


INPROJ_GROUPS = dict(q=0, k=1, v=2, u=3, z=4)


def _gmlp_mix(u, zn, ws_ref, bst_ref, om_ref):
    znb = zn.astype(BF16)
    row = lax.broadcasted_iota(jnp.int32, (MLP_CHUNK, MLP_CHUNK), 0)
    col = lax.broadcasted_iota(jnp.int32, (MLP_CHUNK, MLP_CHUNK), 1)
    lower = col <= row
    for g in range(ws_ref.shape[0]):
        w = jnp.where(lower, ws_ref[g], 0.0).astype(BF16)
        bias = bst_ref[:, g:g + 1]
        cs = slice(g * MLP_GROUP_DIM, (g + 1) * MLP_GROUP_DIM)
        for c in range(znb.shape[0] // MLP_CHUNK):
            rs = slice(c * MLP_CHUNK, (c + 1) * MLP_CHUNK)
            mixed = jnp.dot(w, znb[rs, cs], preferred_element_type=F32) + bias
            om_ref[rs, cs] = (u[rs, cs] * mixed).astype(BF16)


def _inproj_kernel(h_ref, gain_ref, w_ref, vgain_ref, ws_ref, bst_ref,
                   q_ref, k_ref, v_ref, kb_ref, vb_ref, om_ref, *zn_out):
    width = q_ref.shape[1]
    h = h_ref[...]
    n = (h * _rms_scale(h) * gain_ref[...]).astype(BF16)

    def project(name):
        g = INPROJ_GROUPS[name]
        return jnp.dot(n, w_ref[:, g * width:(g + 1) * width], preferred_element_type=F32)

    def store_state(ref, y):
        heads = width // SB_HEAD_DIM
        for hd in range(heads):
            ref[pl.ds(hd, y.shape[0], stride=heads), :] = y[:, hd * SB_HEAD_DIM:(hd + 1) * SB_HEAD_DIM]

    u = jax.nn.gelu(project("u"), approximate=True)
    gz = jax.nn.gelu(project("z"), approximate=True)
    zn = gz * _rms_scale(gz) * vgain_ref[...]
    if zn_out:
        store_state(zn_out[0], zn)
    _gmlp_mix(u, zn, ws_ref, bst_ref, om_ref)
    q_ref[...] = (project("q") * (SB_HEAD_DIM ** -0.5 * LOG2_E)).astype(BF16)
    k = project("k")
    store_state(k_ref, k)
    kb_ref[...] = k.astype(BF16)
    v = project("v")
    store_state(v_ref, v)
    vb_ref[...] = v.astype(BF16)


def _inproj(h, gain, w_in, vgain, ws, bst, *, emit_zn):
    t, d = h.shape
    width = w_in.shape[1] // len(INPROJ_GROUPS)
    tm = min(ROW_TILE, t)
    assert t % tm == 0 and tm % MLP_CHUNK == 0 and w_in.shape[1] == len(INPROJ_GROUPS) * width
    groups = ws.shape[0]
    row_blk = lambda i: (i, 0)
    const2 = lambda i: (0, 0)
    per_row = width // SB_HEAD_DIM
    state = jax.ShapeDtypeStruct((t * per_row, SB_HEAD_DIM), F32)
    out_shape = [
        jax.ShapeDtypeStruct((t, width), BF16),
        state,
        state,
        jax.ShapeDtypeStruct((t, width), BF16),
        jax.ShapeDtypeStruct((t, width), BF16),
        jax.ShapeDtypeStruct((t, width), BF16),
    ]
    if emit_zn:
        out_shape.append(state)
    out_specs = [pl.BlockSpec((tm * per_row, SB_HEAD_DIM), row_blk) if s is state
                 else pl.BlockSpec((tm, width), row_blk) for s in out_shape]
    out_bytes = sum(tm * width * jnp.dtype(s.dtype).itemsize for s in out_shape)
    vmem = (2 * tm * d * 4 + w_in.size * 2 + 2 * out_bytes + tm * d * 2 + 6 * tm * width * 4)
    return pl.pallas_call(
        _inproj_kernel,
        grid=(t // tm,),
        in_specs=[
            pl.BlockSpec((tm, d), row_blk),
            pl.BlockSpec((1, d), const2),
            pl.BlockSpec(w_in.shape, const2, pipeline_mode=pl.Buffered(1)),
            pl.BlockSpec((1, width), const2),
            pl.BlockSpec((groups, MLP_CHUNK, MLP_CHUNK), lambda i: (0, 0, 0)),
            pl.BlockSpec((MLP_CHUNK, groups), const2),
        ],
        out_specs=out_specs,
        out_shape=out_shape,
        compiler_params=_compiler_params(("parallel",), vmem),
        name="inproj",
    )(h, gain.reshape(1, d), w_in, vgain.reshape(1, width), ws, bst)


def _init_tri(tri_ref):
    j = lax.broadcasted_iota(jnp.int32, tri_ref.shape, 0)
    s = lax.broadcasted_iota(jnp.int32, tri_ref.shape, 1)
    tri_ref[...] = jnp.where(j > s, -1.0, 0.0).astype(BF16)


def _sb_span(q, k, v, valids, tri, carry):
    tb = tri.shape[1]
    u = lax.dot_general(q, k, (((1,), (1,)), ((), ())), preferred_element_type=F32)
    s = jnp.maximum(u, 0.0) + jnp.log2(1.0 + jnp.exp2(jnp.minimum(u, -u)))
    weights = [None] * len(valids)
    for b in reversed(range(len(valids))):
        cols = slice(b * tb, (b + 1) * tb)
        sb = s[:, cols]
        if valids[b] is not None:
            sb = jnp.where(valids[b], sb, 0.0)
        newer = jnp.dot(sb.astype(BF16), tri, preferred_element_type=F32)
        log2_a = u[:, cols] - sb + newer
        total = jnp.sum(sb, axis=1, keepdims=True)
        if carry is None:
            carry = total
        else:
            log2_a = log2_a - carry
            carry = carry + total
        a = jnp.exp2(log2_a)
        if valids[b] is not None:
            a = jnp.where(valids[b], a, 0.0)
        weights[b] = a.astype(BF16)
    a = weights[0] if len(weights) == 1 else jnp.concatenate(weights, axis=1)
    return jnp.dot(a, v, preferred_element_type=F32), carry


def _head_rows(ref, h, rows):
    return ref.at[pl.ds(h * rows, rows)]


def _sb_older_blocks(q, load_kv, first_kb, tri, carry_ref, acc_ref):
    def cond(state):
        kb, low = state
        return (kb >= 0) & (low < DEAD_LOG2)

    def body(state):
        kb, _ = state
        k, v = load_kv(kb)
        o, carry = _sb_span(q, k, v, [None], tri, carry_ref[...])
        acc_ref[...] += o
        carry_ref[...] = carry
        return kb - 1, jnp.min(carry)

    return lax.while_loop(cond, body, (first_kb, jnp.min(carry_ref[...])))[1]


def _attn_prompt_kernel(q_ref, k_ref, v_ref, o_ref, tri_ref, carry_ref, acc_ref, *, older_only):
    i = pl.program_id(1)
    tb = tri_ref.shape[0]
    nqb = q_ref.shape[0] // tb
    heads = q_ref.shape[1] // SB_HEAD_DIM

    @pl.when((pl.program_id(0) == 0) & (i == 0))
    def _():
        _init_tri(tri_ref)

    tri = tri_ref[...]
    causal = (lax.broadcasted_iota(jnp.int32, (tb, tb), 1)
              < lax.broadcasted_iota(jnp.int32, (tb, tb), 0))
    chains = [(b, h) for b in range(nqb) for h in range(heads)]

    def q_rows(b):
        return slice(b * tb, (b + 1) * tb)

    def head_cols(h):
        return slice(h * SB_HEAD_DIM, (h + 1) * SB_HEAD_DIM)

    def first_span(b, h, with_previous_block):
        hc = head_cols(h)
        qblock = i * nqb + b
        if with_previous_block:
            keys = pl.ds(pl.multiple_of((qblock - 1) * tb, tb), 2 * tb)
            valids = [None, causal]
        else:
            keys = pl.ds(pl.multiple_of(qblock * tb, tb), tb)
            valids = [causal]
        o, carry = _sb_span(q_ref[q_rows(b), hc], k_ref[keys, hc], v_ref[keys, hc], valids,
                            tri, None)
        slot = b * heads + h
        if older_only:
            o = jnp.zeros_like(o)
        o_ref[q_rows(b), hc] = o.astype(o_ref.dtype)
        _head_rows(acc_ref, slot, tb)[...] = o
        _head_rows(carry_ref, slot, tb)[...] = carry
        return carry

    def run(first_block_has_previous):
        lowest = None
        for b, h in chains:
            carry = first_span(b, h, first_block_has_previous or b > 0)
            if first_block_has_previous or b >= 2:
                lowest = carry if lowest is None else jnp.minimum(lowest, carry)
        if lowest is None:
            return

        @pl.when(jnp.min(lowest) < DEAD_LOG2)
        def _():
            for b, h in chains:
                hc = head_cols(h)
                slot = b * heads + h

                def load_kv(kb, hc=hc):
                    rows = pl.ds(pl.multiple_of(kb * tb, tb), tb)
                    return k_ref[rows, hc], v_ref[rows, hc]

                acc = _head_rows(acc_ref, slot, tb)
                _sb_older_blocks(q_ref[q_rows(b), hc], load_kv, i * nqb + b - 2, tri,
                                 _head_rows(carry_ref, slot, tb), acc)
                o_ref[q_rows(b), hc] = acc[...].astype(o_ref.dtype)

    @pl.when(i == 0)
    def _():
        run(False)

    @pl.when(i > 0)
    def _():
        run(True)


def _attn_prompt(q, k, v, *, older_only=False):
    s, width = q.shape
    hb = ATT_HEADS_PER_STEP
    tb = ATT_BLOCK
    tq = ATT_QBLOCKS_PER_STEP * tb
    assert s % tq == 0 and width % (hb * SB_HEAD_DIM) == 0
    wb = hb * SB_HEAD_DIM
    chains = ATT_QBLOCKS_PER_STEP * hb
    resident = pl.Buffered(1)
    vmem = 2 * s * wb * 2 + 2 * 2 * tq * wb * 2 + tb * tb * 2 + chains * 8 * tb * 2 * tb * 4
    return pl.pallas_call(
        functools.partial(_attn_prompt_kernel, older_only=older_only),
        grid=(width // wb, s // tq),
        in_specs=[
            pl.BlockSpec((tq, wb), lambda g, i: (i, g)),
            pl.BlockSpec((s, wb), lambda g, i: (0, g), pipeline_mode=resident),
            pl.BlockSpec((s, wb), lambda g, i: (0, g), pipeline_mode=resident),
        ],
        out_specs=pl.BlockSpec((tq, wb), lambda g, i: (i, g)),
        out_shape=jax.ShapeDtypeStruct((s, width), BF16),
        scratch_shapes=[
            pltpu.VMEM((tb, tb), BF16),
            pltpu.VMEM((chains * tb, 1), F32),
            pltpu.VMEM((chains * tb, SB_HEAD_DIM), F32),
        ],
        compiler_params=_compiler_params(("arbitrary", "arbitrary"), vmem),
        name="attn_prompt",
    )(q, k, v)


def _attn_sample_kernel(q_ref, kn_ref, vn_ref, ck_ref, cv_ref, o_ref, *rest, heads, older_only):
    if older_only:
        tri_ref, carry_ref, acc_ref = rest
    else:
        low_ref, tri_ref = rest
    tq = q_ref.shape[0]
    held = ck_ref.shape[1] // heads
    tb = tri_ref.shape[1]
    head_rows = tb - tq
    first = held - head_rows

    @pl.when(pl.program_id(0) == 0)
    def _():
        _init_tri(tri_ref)

    tri = tri_ref[...]
    row = lax.broadcasted_iota(jnp.int32, (tq, tb), 0)
    col = lax.broadcasted_iota(jnp.int32, (tq, tb), 1)

    def cache_rows(ref, h, start, size):
        return ref[0, pl.ds(start * heads + h, size, stride=heads), :].astype(BF16)

    lowest = None
    for h in range(heads):
        hc = slice(h * SB_HEAD_DIM, (h + 1) * SB_HEAD_DIM)
        k0 = jnp.concatenate([cache_rows(ck_ref, h, first, head_rows), kn_ref[:, hc]], axis=0)
        v0 = jnp.concatenate([cache_rows(cv_ref, h, first, head_rows), vn_ref[:, hc]], axis=0)
        o, carry = _sb_span(q_ref[:, hc], k0, v0, [col < row + head_rows], tri, None)
        if older_only:
            o = jnp.zeros_like(o)
            _head_rows(acc_ref, h, tq)[...] = o
            _head_rows(carry_ref, h, tq)[...] = carry
        o_ref[:, hc] = o.astype(o_ref.dtype)
        lowest = carry if lowest is None else jnp.minimum(lowest, carry)

    if not older_only:
        low_ref[...] = jnp.broadcast_to(jnp.min(lowest, axis=0, keepdims=True), low_ref.shape)
        return

    nfull = first // tb
    rem = first - nfull * tb

    @pl.when(jnp.min(lowest) < DEAD_LOG2)
    def _():
        for h in range(heads):
            hc = slice(h * SB_HEAD_DIM, (h + 1) * SB_HEAD_DIM)
            q = q_ref[:, hc]
            acc = _head_rows(acc_ref, h, tq)
            carry_h = _head_rows(carry_ref, h, tq)

            def load_kv(kb, h=h):
                start = rem + kb * tb
                return cache_rows(ck_ref, h, start, tb), cache_rows(cv_ref, h, start, tb)

            low = _sb_older_blocks(q, load_kv, nfull - 1, tri, carry_h, acc)

            if rem:
                @pl.when(low < DEAD_LOG2)
                def _():
                    o, _ = _sb_span(q, cache_rows(ck_ref, h, 0, tb), cache_rows(cv_ref, h, 0, tb),
                                    [col < rem], tri, carry_h[...])
                    acc[...] += o

            o_ref[:, hc] = acc[...].astype(o_ref.dtype)


def _attn_sample(q, kn, vn, cache_k, cache_v, batch, heads):
    rows, width = q.shape
    tq = rows // batch
    past = cache_k.shape[1] // heads
    tb = ATT_BLOCK
    assert tq < tb and past >= 2 * tb and past % tb == 0 and tq % 16 == 0
    assert width == heads * SB_HEAD_DIM
    new_blk = lambda b: (b, 0)
    new_specs = [pl.BlockSpec((tq, width), new_blk)] * 3

    def call(older_only):
        held = past if older_only else tb
        cache_spec = pl.BlockSpec((1, held * heads, SB_HEAD_DIM),
                                  lambda b: (b, past // held - 1, 0))
        out_specs = [pl.BlockSpec((tq, width), new_blk)]
        out_shape = [jax.ShapeDtypeStruct((rows, width), BF16)]
        scratch = [pltpu.VMEM((tb, tb), BF16)]
        if older_only:
            scratch += [pltpu.VMEM((heads * tq, 1), F32),
                        pltpu.VMEM((heads * tq, SB_HEAD_DIM), F32)]
        else:
            out_specs.append(pl.BlockSpec((SUBLANES, LANES), new_blk))
            out_shape.append(jax.ShapeDtypeStruct((batch * SUBLANES, LANES), F32))
        vmem = 2 * 2 * held * width * 4 + tb * tb * 2 + heads * 14 * tq * tb * 4
        return pl.pallas_call(
            functools.partial(_attn_sample_kernel, heads=heads, older_only=older_only),
            grid=(batch,),
            in_specs=new_specs + [cache_spec, cache_spec],
            out_specs=out_specs,
            out_shape=out_shape,
            scratch_shapes=scratch,
            compiler_params=_compiler_params(("arbitrary",), vmem),
            name="attn_sample_older" if older_only else "attn_sample",
        )(q, kn, vn, cache_k, cache_v)

    o, low = call(False)

    def add_older_rows(o):
        older = call(True)[0]
        return (o.astype(F32) + older.astype(F32)).astype(BF16)

    return lax.cond(jnp.any(low < DEAD_LOG2), add_older_rows, lambda o: o, o)


def _outproj_kernel(h_ref, osb_ref, om_ref, wa_ref, wb_ref, o_ref):
    o_ref[...] = (h_ref[...]
                  + jnp.dot(osb_ref[...], wa_ref[...], preferred_element_type=F32)
                  + jnp.dot(om_ref[...], wb_ref[...], preferred_element_type=F32))


def _outproj(h, osb, om, w_out):
    t, d = h.shape
    width = osb.shape[1]
    tm = min(ROW_TILE, t)
    assert t % tm == 0 and w_out.shape == (2 * width, d)
    vmem = 2 * 2 * tm * d * 4 + 2 * 2 * tm * width * 2 + 2 * 2 * width * d * 2
    return pl.pallas_call(
        _outproj_kernel,
        grid=(t // tm,),
        in_specs=[
            pl.BlockSpec((tm, d), lambda i: (i, 0)),
            pl.BlockSpec((tm, width), lambda i: (i, 0)),
            pl.BlockSpec((tm, width), lambda i: (i, 0)),
            pl.BlockSpec((width, d), lambda i: (0, 0)),
            pl.BlockSpec((width, d), lambda i: (1, 0)),
        ],
        out_specs=pl.BlockSpec((tm, d), lambda i: (i, 0)),
        out_shape=jax.ShapeDtypeStruct((t, d), F32),
        compiler_params=_compiler_params(("parallel",), vmem),
        name="outproj",
    )(h, osb, om, w_out, w_out)


def _mix_out_kernel(q_ref, kc_ref, vc_ref, kp_ref, vp_ref, h_ref, om_ref, w_ref,
                    o_ref, low_ref, tri_ref, osb0_ref, osb1_ref, *, nt):
    s = pl.program_id(0)
    tb = tri_ref.shape[0]
    width = q_ref.shape[1]
    heads = width // SB_HEAD_DIM
    nqb = q_ref.shape[0] // tb

    @pl.when(s == 0)
    def _():
        _init_tri(tri_ref)

    def project_columns(osb_ref, c, nchunks):
        cols = slice(c * (o_ref.shape[1] // nchunks), (c + 1) * (o_ref.shape[1] // nchunks))
        o_ref[:, cols] = (h_ref[:, cols]
                          + jnp.dot(osb_ref[...], w_ref[0:width, cols], preferred_element_type=F32)
                          + jnp.dot(om_ref[...], w_ref[width:, cols], preferred_element_type=F32))

    def attend(osb_ref, first_tile, project_from=None):
        tri = tri_ref[...]
        causal = (lax.broadcasted_iota(jnp.int32, (tb, tb), 1)
                  < lax.broadcasted_iota(jnp.int32, (tb, tb), 0))
        lowest = None
        nchunks = nqb * heads // 2
        for b in range(nqb):
            rows = slice(b * tb, (b + 1) * tb)
            for h in range(heads):
                if project_from is not None and (b * heads + h) % 2 == 0:
                    project_columns(project_from, (b * heads + h) // 2, nchunks)
                hc = slice(h * SB_HEAD_DIM, (h + 1) * SB_HEAD_DIM)
                if b > 0:
                    keys = slice((b - 1) * tb, (b + 1) * tb)
                    k, v, valids = kc_ref[keys, hc], vc_ref[keys, hc], [None, causal]
                elif first_tile:
                    k, v, valids = kc_ref[0:tb, hc], vc_ref[0:tb, hc], [causal]
                else:
                    k = jnp.concatenate([kp_ref[:, hc], kc_ref[0:tb, hc]], axis=0)
                    v = jnp.concatenate([vp_ref[:, hc], vc_ref[0:tb, hc]], axis=0)
                    valids = [None, causal]
                o, carry = _sb_span(q_ref[rows, hc], k, v, valids, tri, None)
                osb_ref[rows, hc] = o.astype(BF16)
                if not (first_tile and b < 2):
                    lowest = carry if lowest is None else jnp.minimum(lowest, carry)
        if lowest is None:
            low_ref[...] = jnp.full(low_ref.shape, DEAD_LOG2, F32)
        else:
            low_ref[...] = jnp.broadcast_to(jnp.min(lowest, axis=0, keepdims=True),
                                            low_ref.shape)

    def project(osb_ref):
        o_ref[...] = (h_ref[...]
                      + jnp.dot(osb_ref[...], w_ref[0:width], preferred_element_type=F32)
                      + jnp.dot(om_ref[...], w_ref[width:], preferred_element_type=F32))

    @pl.when(s == 0)
    def _():
        attend(osb0_ref, True)

    middle = (s > 0) & (s < nt)

    @pl.when(middle & (s % 2 == 1))
    def _():
        attend(osb1_ref, False, project_from=osb0_ref)

    @pl.when(middle & (s % 2 == 0))
    def _():
        attend(osb0_ref, False, project_from=osb1_ref)

    @pl.when(s == nt)
    def _():
        project(osb1_ref if (nt - 1) % 2 else osb0_ref)


def _mix_out(h, q, kb, vb, om, w_out):
    t, d = h.shape
    width = q.shape[1]
    tb = ATT_BLOCK
    tm = ATT_QBLOCKS_PER_STEP * tb
    assert t % tm == 0 and w_out.shape == (2 * width, d)
    nt = t // tm
    sub = tm // tb
    cur = lambda s: (jnp.minimum(s, nt - 1), 0)
    prev_blk = lambda s: (jnp.maximum(jnp.minimum(s, nt - 1) * sub - 1, 0), 0)
    done = lambda s: (jnp.maximum(s - 1, 0), 0)
    vmem = (2 * (3 * tm + 2 * tb) * width * 2 + 2 * 2 * tm * d * 4 + 2 * tm * width * 2
            + w_out.size * 2 + 2 * tm * width * 2 + 16 * 8 * tb * 2 * tb * 4 // 4)
    return pl.pallas_call(
        functools.partial(_mix_out_kernel, nt=nt),
        grid=(nt + 1,),
        in_specs=[
            pl.BlockSpec((tm, width), cur),
            pl.BlockSpec((tm, width), cur),
            pl.BlockSpec((tm, width), cur),
            pl.BlockSpec((tb, width), prev_blk),
            pl.BlockSpec((tb, width), prev_blk),
            pl.BlockSpec((tm, d), done),
            pl.BlockSpec((tm, width), done),
            pl.BlockSpec(w_out.shape, lambda s: (0, 0), pipeline_mode=pl.Buffered(1)),
        ],
        out_specs=[pl.BlockSpec((tm, d), done), pl.BlockSpec((SUBLANES, LANES), cur)],
        out_shape=[jax.ShapeDtypeStruct((t, d), F32),
                   jax.ShapeDtypeStruct((nt * SUBLANES, LANES), F32)],
        scratch_shapes=[pltpu.VMEM((tb, tb), BF16), pltpu.VMEM((tm, width), BF16),
                        pltpu.VMEM((tm, width), BF16)],
        compiler_params=_compiler_params(("arbitrary",), vmem),
        name="mix_out",
    )(q, kb, vb, kb, vb, h, om, w_out)


def _ple_kernel(h_ref, p_ref, gain_ref, wg_ref, wp_ref, fgain_ref, o_ref, *, final_norm):
    h = h_ref[...]
    n = (h * _rms_scale(h) * gain_ref[...]).astype(BF16)
    gate = jax.nn.sigmoid(jnp.dot(n, wg_ref[...], preferred_element_type=F32))
    proj = jnp.dot(p_ref[...].astype(BF16), wp_ref[...], preferred_element_type=F32)
    o = h + gate * proj
    if final_norm:
        o = o * _rms_scale(o) * fgain_ref[...]
    o_ref[...] = o


def _ple(h, p, gain, w_gate, w_proj, fgain, *, final_norm):
    t, d = h.shape
    pd = p.shape[1]
    tm = min(ROW_TILE, t)
    assert t % tm == 0
    vmem = 2 * 2 * tm * d * 4 + 2 * tm * pd * 4 + 2 * d * d * 2 + 2 * pd * d * 2 + 3 * tm * d * 4
    return pl.pallas_call(
        functools.partial(_ple_kernel, final_norm=final_norm),
        grid=(t // tm,),
        in_specs=[
            pl.BlockSpec((tm, d), lambda i: (i, 0)),
            pl.BlockSpec((tm, pd), lambda i: (i, 0)),
            pl.BlockSpec((1, d), lambda i: (0, 0)),
            pl.BlockSpec((d, d), lambda i: (0, 0)),
            pl.BlockSpec((pd, d), lambda i: (0, 0)),
            pl.BlockSpec((1, d), lambda i: (0, 0)),
        ],
        out_specs=pl.BlockSpec((tm, d), lambda i: (i, 0)),
        out_shape=jax.ShapeDtypeStruct((t, d), F32),
        compiler_params=_compiler_params(("parallel",), vmem),
        name="ple",
    )(h, p, gain.reshape(1, d), w_gate, w_proj, fgain.reshape(1, d))


def _gmlp_weights(ws, bs, seq_len):
    length = min(seq_len, MLP_CHUNK)
    assert MLP_CHUNK % length == 0
    reps = MLP_CHUNK // length
    if reps > 1:
        eye = jnp.eye(reps, dtype=ws.dtype)
        ws = jnp.einsum("ab,gij->gaibj", eye, ws[:, :length, :length]).reshape(
            ws.shape[0], MLP_CHUNK, MLP_CHUNK)
        bs = jnp.tile(bs[:, :length], (1, reps))
    return ws, jnp.transpose(bs)


def _layer(x, p, w, *, seq_len, final_gain, cache=None):
    rows = x.shape[0]
    w = dict(w)
    h, (w["ffn1_wg"], w["ffn1_wu"], w["ffn1_wd"]) = _ffn(
        x, w["ffn1_norm"], w["ffn1_wg"], w["ffn1_wu"], w["ffn1_wd"])
    ws, bst = _gmlp_weights(w["gmlp_ws"], w["gmlp_bs"], seq_len)
    outs = _inproj(h, w["mix_norm"], w["w_in"], w["gmlp_v_norm"], ws, bst,
                   emit_zn=cache is not None)
    q, k, v, kb, vb, om = outs[:6]
    if cache is None:
        h, low = _mix_out(h, q, kb, vb, om, w["w_out"])

        def finish_older_keys(h):
            older = _attn_prompt(q, kb, vb, older_only=True)
            return _outproj(h, older, jnp.zeros_like(om), w["w_out"])

        h = lax.cond(jnp.any(low < DEAD_LOG2), finish_older_keys, lambda h: h, h)
        zn = None
    else:
        osb = _attn_sample(q, kb, vb, cache[0], cache[1], rows // seq_len,
                           q.shape[1] // SB_HEAD_DIM)
        zn = outs[6]
        h = _outproj(h, osb, om, w["w_out"])
    h, (w["ffn2_wg"], w["ffn2_wu"], w["ffn2_wd"]) = _ffn(
        h, w["ffn2_norm"], w["ffn2_wg"], w["ffn2_wu"], w["ffn2_wd"])
    h = _ple(h, p, w["ple_norm"], w["ple_w_gate"], w["ple_w_proj"],
             final_gain if final_gain is not None else w["ple_norm"],
             final_norm=final_gain is not None)
    return h, k, v, zn, w


_PRECAST_WEIGHTS = ("w_in", "w_out", "ple_w_gate", "ple_w_proj")


def kernel(x_prompt, x_sample, cache_k, cache_v, p_prompt, p_sample, ffn1_norm, ffn1_wg, ffn1_wu, ffn1_wd, mix_norm, w_in, gmlp_v_norm, gmlp_ws, gmlp_bs, w_out, ffn2_norm, ffn2_wg, ffn2_wu, ffn2_wd, ple_norm, ple_w_gate, ple_w_proj, final_norm):
    weights = dict(ffn1_norm=ffn1_norm, ffn1_wg=ffn1_wg, ffn1_wu=ffn1_wu, ffn1_wd=ffn1_wd,
                   mix_norm=mix_norm, w_in=w_in, gmlp_v_norm=gmlp_v_norm, gmlp_ws=gmlp_ws,
                   gmlp_bs=gmlp_bs, w_out=w_out, ffn2_norm=ffn2_norm, ffn2_wg=ffn2_wg,
                   ffn2_wu=ffn2_wu, ffn2_wd=ffn2_wd, ple_norm=ple_norm,
                   ple_w_gate=ple_w_gate, ple_w_proj=ple_w_proj)
    depth = w_in.shape[0]
    bp, sp, d = x_prompt.shape
    bs, ss, _ = x_sample.shape
    assert bp == 1, "prompt attention handles one sequence"
    heads = cache_k.shape[3]
    width = heads * SB_HEAD_DIM

    hp = x_prompt.reshape(bp * sp, d)
    hs = x_sample.reshape(bs * ss, d)
    kp_l, vp_l, ks_l, vs_l, zs_l = [], [], [], [], []
    for i in range(depth):
        w = {name: (val[i].astype(BF16) if name in _PRECAST_WEIGHTS else val[i])
             for name, val in weights.items()}
        fgain = final_norm if i == depth - 1 else None
        cache = (cache_k[i].reshape(bs, -1, SB_HEAD_DIM), cache_v[i].reshape(bs, -1, SB_HEAD_DIM))
        hs, k_s, v_s, z_s, w = _layer(hs, p_sample[i].reshape(bs * ss, -1), w, seq_len=ss,
                                      final_gain=fgain, cache=cache)
        hp, kp, vp, _, _ = _layer(hp, p_prompt[i].reshape(bp * sp, -1), w, seq_len=sp,
                                  final_gain=fgain)
        kp_l.append(kp.reshape(bp, sp, heads, SB_HEAD_DIM))
        vp_l.append(vp.reshape(bp, sp, heads, SB_HEAD_DIM))
        ks_l.append(k_s.reshape(bs, ss, heads, SB_HEAD_DIM))
        vs_l.append(v_s.reshape(bs, ss, heads, SB_HEAD_DIM))
        zs_l.append(z_s.reshape(bs, ss, -1, MLP_GROUP_DIM))
    return (hp.reshape(bp, sp, d), hs.reshape(bs, ss, d),
            jnp.stack(kp_l), jnp.stack(vp_l), jnp.stack(ks_l), jnp.stack(vs_l), jnp.stack(zs_l))
```
